```python
import jax, jax.numpy as jnp
from jax import lax
import numpy as np

D_MODEL = 1024
BATCH = 4
SEQ = 8192
DEPTH = 2

D_CONV = 512
CONV_WIDTH = 31
D_POOL = 512
POOL_WINDOWS = (2, 4, 8, 16)
N_POOL_GROUPS = 4
POOL_GROUP = D_POOL // N_POOL_GROUPS
POOL_OUT_GROUP = D_MODEL // N_POOL_GROUPS
D_SGU = 512
SGU_HEADS = 4
SGU_HEAD_DIM = D_SGU // SGU_HEADS
CHUNK = 128
N_BRANCHES = 3
SPLIT_A = 2 * D_CONV
SPLIT_B = SPLIT_A + D_POOL
SPLIT_U = SPLIT_B + D_SGU
SPLIT_V = SPLIT_U + D_SGU
D_IN = SPLIT_V + N_BRANCHES * D_MODEL
D_FF = 2816
N_EXPERTS = 8
TOP_K = 2
D_EXPERT = 3584
MOE_BLOCK = 512
N_DENSE = (DEPTH + 1) // 2
N_MOE = DEPTH // 2
EPS = 1e-6

kernel_name = 'hybrid_conv_pool_sgu_moe_block'


def rmsnorm(x, g):
    xf = x.astype(jnp.float32)
    y = xf * lax.rsqrt(jnp.mean(xf * xf, axis=-1, keepdims=True) + EPS)
    return (y * g.astype(jnp.float32)).astype(x.dtype)


def layernorm(x, g, b):
    xf = x.astype(jnp.float32)
    mu = jnp.mean(xf, axis=-1, keepdims=True)
    xc = xf - mu
    var = jnp.mean(xc * xc, axis=-1, keepdims=True)
    y = xc * lax.rsqrt(var + EPS) * g.astype(jnp.float32) + b.astype(jnp.float32)
    return y.astype(x.dtype)


def conformer_conv(a_in, w_dw, b_dw, ln_g, ln_b, w_o):
    val, gate = jnp.split(a_in, 2, axis=-1)
    a = val * jax.nn.sigmoid(gate)
    a = lax.conv_general_dilated(
        a, w_dw[:, None, :], window_strides=(1,),
        padding=[(CONV_WIDTH - 1, 0)],
        dimension_numbers=('NWC', 'WIO', 'NWC'),
        feature_group_count=D_CONV) + b_dw
    a = jax.nn.silu(layernorm(a, ln_g, ln_b))
    return a @ w_o


def multiscale_pool(p, w_grp, scale):
    b_, s_, _ = p.shape
    pf = p.astype(jnp.float32)
    csum = jnp.cumsum(pf, axis=1)
    count = jnp.arange(1, s_ + 1, dtype=jnp.float32)[:, None]
    outs = []
    for g, w in enumerate(POOL_WINDOWS):
        sl = slice(g * POOL_GROUP, (g + 1) * POOL_GROUP)
        c = csum[..., sl]
        c_prev = jnp.pad(c, ((0, 0), (w, 0), (0, 0)))[:, :s_]
        mean = (c - c_prev) / jnp.minimum(count, float(w))
        outs.append(mean - pf[..., sl])
    pooled = jnp.stack(outs, axis=2).astype(p.dtype)
    y = jnp.einsum('bsgc,gcd->bsgd', pooled, w_grp).reshape(b_, s_, D_MODEL)
    return y * scale


def spatial_gating(u, v, ln_g, ln_b, w_s, b_s, w_o):
    b_, s_, _ = u.shape
    n_chunks = s_ // CHUNK
    vn = layernorm(v, ln_g, ln_b).reshape(b_, n_chunks, CHUNK, SGU_HEADS, SGU_HEAD_DIM)
    causal = jnp.tril(jnp.ones((CHUNK, CHUNK), dtype=bool))
    w = jnp.where(causal, w_s, 0)
    s = jnp.einsum('hij,bcjhd->bcihd', w, vn) + b_s.T[:, :, None]
    s = s.reshape(b_, s_, D_SGU)
    return (u * s) @ w_o


def swiglu(h, w1, w3, w2):
    return (jax.nn.silu(h @ w1) * (h @ w3)) @ w2


def moe_swiglu(h, w_router, w1, w3, w2):
    b_, s_, d = h.shape
    n_tok = b_ * s_
    n_assign = n_tok * TOP_K
    ht = h.reshape(n_tok, d)
    logits = (ht @ w_router).astype(jnp.float32)
    top_v, top_i = lax.top_k(logits, TOP_K)
    top_w = jax.nn.softmax(top_v, axis=-1)
    flat_e = top_i.reshape(-1)
    flat_w = top_w.reshape(-1)
    flat_tok = jnp.repeat(jnp.arange(n_tok, dtype=jnp.int32), TOP_K)
    order = jnp.argsort(flat_e)
    sorted_e = flat_e[order]
    counts = jnp.zeros((N_EXPERTS,), jnp.int32).at[flat_e].add(1)
    padded = (counts + MOE_BLOCK - 1) // MOE_BLOCK * MOE_BLOCK
    start = jnp.cumsum(counts) - counts
    pend = jnp.cumsum(padded)
    pstart = pend - padded
    rank = jnp.arange(n_assign, dtype=jnp.int32) - start[sorted_e]
    dest = pstart[sorted_e] + rank
    n_rows = -(-n_assign // MOE_BLOCK) * MOE_BLOCK + N_EXPERTS * MOE_BLOCK
    n_blocks = n_rows // MOE_BLOCK
    row_tok = jnp.zeros((n_rows,), jnp.int32).at[dest].set(flat_tok[order])
    row_w = jnp.zeros((n_rows,), jnp.float32).at[dest].set(flat_w[order])
    block_start = jnp.arange(n_blocks, dtype=jnp.int32) * MOE_BLOCK
    block_e = jnp.minimum(jnp.searchsorted(pend, block_start, side='right'), N_EXPERTS - 1)
    xb = ht[row_tok].reshape(n_blocks, MOE_BLOCK, d)

    def expert_block(args):
        xblk, e = args
        return swiglu(xblk, w1[e], w3[e], w2[e])

    yb = lax.map(expert_block, (xb, block_e)).reshape(n_rows, d)
    out = jnp.zeros((n_tok, d), jnp.float32).at[row_tok].add(yb.astype(jnp.float32) * row_w[:, None])
    return out.astype(h.dtype).reshape(b_, s_, d)


def setup_inputs(seed: int = 0) -> dict:
    key = jax.random.key(seed)
    ks = jax.random.split(key, 32)
    f32 = jnp.float32

    def nrm(k, shape, scale):
        return jax.random.normal(k, shape, f32) * scale

    def gain(k, shape):
        return 1.0 + 0.02 * jax.random.normal(k, shape, f32)

    return {
        'x': jax.random.normal(ks[0], (BATCH, SEQ, D_MODEL), f32),
        'mix_norm_g': gain(ks[1], (DEPTH, D_MODEL)),
        'w_in': nrm(ks[2], (DEPTH, D_MODEL, D_IN), D_MODEL ** -0.5),
        'conv_w': nrm(ks[3], (DEPTH, CONV_WIDTH, D_CONV), CONV_WIDTH ** -0.5),
        'conv_b': nrm(ks[4], (DEPTH, D_CONV), 0.02),
        'conv_ln_g': gain(ks[5], (DEPTH, D_CONV)),
        'conv_ln_b': nrm(ks[6], (DEPTH, D_CONV), 0.02),
        'conv_w_out': nrm(ks[7], (DEPTH, D_CONV, D_MODEL), D_CONV ** -0.5),
        'pool_w': nrm(ks[8], (DEPTH, N_POOL_GROUPS, POOL_GROUP, POOL_OUT_GROUP), POOL_GROUP ** -0.5),
        'pool_scale': gain(ks[9], (DEPTH, D_MODEL)),
        'sgu_ln_g': gain(ks[10], (DEPTH, D_SGU)),
        'sgu_ln_b': nrm(ks[11], (DEPTH, D_SGU), 0.02),
        'sgu_w_s': nrm(ks[12], (DEPTH, SGU_HEADS, CHUNK, CHUNK), CHUNK ** -0.5),
        'sgu_b_s': gain(ks[13], (DEPTH, SGU_HEADS, CHUNK)),
        'sgu_w_out': nrm(ks[14], (DEPTH, D_SGU, D_MODEL), D_SGU ** -0.5),
        'w_out': nrm(ks[15], (DEPTH, D_MODEL, D_MODEL), D_MODEL ** -0.5),
        'ffn_norm_g': gain(ks[16], (DEPTH, D_MODEL)),
        'dense_w1': nrm(ks[17], (N_DENSE, D_MODEL, D_FF), D_MODEL ** -0.5),
        'dense_w3': nrm(ks[18], (N_DENSE, D_MODEL, D_FF), D_MODEL ** -0.5),
        'dense_w2': nrm(ks[19], (N_DENSE, D_FF, D_MODEL), D_FF ** -0.5),
        'router_w': nrm(ks[20], (N_MOE, D_MODEL, N_EXPERTS), D_MODEL ** -0.5),
        'expert_w1': nrm(ks[21], (N_MOE, N_EXPERTS, D_MODEL, D_EXPERT), D_MODEL ** -0.5),
        'expert_w3': nrm(ks[22], (N_MOE, N_EXPERTS, D_MODEL, D_EXPERT), D_MODEL ** -0.5),
        'expert_w2': nrm(ks[23], (N_MOE, N_EXPERTS, D_EXPERT, D_MODEL), D_EXPERT ** -0.5),
        'final_norm_g': gain(ks[24], (D_MODEL,)),
    }


def reference(x, mix_norm_g, w_in, conv_w, conv_b, conv_ln_g, conv_ln_b, conv_w_out,
              pool_w, pool_scale, sgu_ln_g, sgu_ln_b, sgu_w_s, sgu_b_s, sgu_w_out,
              w_out, ffn_norm_g, dense_w1, dense_w3, dense_w2,
              router_w, expert_w1, expert_w3, expert_w2, final_norm_g):
    for layer in range(DEPTH):
        h = rmsnorm(x, mix_norm_g[layer])
        proj = h @ w_in[layer]
        a_in, p_in, u, v, gate_logits = jnp.split(proj, [SPLIT_A, SPLIT_B, SPLIT_U, SPLIT_V], axis=-1)
        gates = jax.nn.sigmoid(gate_logits)
        y_a = conformer_conv(a_in, conv_w[layer], conv_b[layer], conv_ln_g[layer],
                             conv_ln_b[layer], conv_w_out[layer])
        y_b = multiscale_pool(p_in, pool_w[layer], pool_scale[layer])
        y_c = spatial_gating(u, v, sgu_ln_g[layer], sgu_ln_b[layer], sgu_w_s[layer],
                             sgu_b_s[layer], sgu_w_out[layer])
        mixed = (gates[..., :D_MODEL] * y_a
                 + gates[..., D_MODEL:2 * D_MODEL] * y_b
                 + gates[..., 2 * D_MODEL:] * y_c)
        x = x + mixed @ w_out[layer]
        h = rmsnorm(x, ffn_norm_g[layer])
        idx = layer // 2
        if layer % 2 == 0:
            x = x + swiglu(h, dense_w1[idx], dense_w3[idx], dense_w2[idx])
        else:
            x = x + moe_swiglu(h, router_w[idx], expert_w1[idx], expert_w3[idx], expert_w2[idx])
    return rmsnorm(x, final_norm_g)
```

```python
import functools

import jax
import jax.numpy as jnp
from jax import lax
from jax.experimental import pallas as pl
from jax.experimental.pallas import tpu as pltpu

F32 = jnp.float32
BF16 = jnp.bfloat16

D_MODEL = 1024
DEPTH = 2
D_CONV = 512
CONV_WIDTH = 31
D_POOL = 512
POOL_WINDOWS = (2, 4, 8, 16)
N_POOL_GROUPS = 4
POOL_GROUP = D_POOL // N_POOL_GROUPS
POOL_OUT_GROUP = D_MODEL // N_POOL_GROUPS
D_SGU = 512
SGU_HEADS = 4
SGU_HEAD_DIM = D_SGU // SGU_HEADS
CHUNK = 128
SPLIT_A = 2 * D_CONV
SPLIT_B = SPLIT_A + D_POOL
SPLIT_U = SPLIT_B + D_SGU
SPLIT_V = SPLIT_U + D_SGU
GATE_A = SPLIT_V
GATE_B = GATE_A + D_MODEL
GATE_C = GATE_B + D_MODEL
D_IN = GATE_C + D_MODEL
D_FF = 2816
N_EXPERTS = 8
TOP_K = 2
D_EXPERT = 3584
MOE_BLOCK = 512
EPS = 1e-6

SUBLANES = 8
HALO = 32
SEQ_TILE = 512
CONV_ROWS = 32
FFN_TILE = 512
ROUTER_TILE = 512
COMBINE_TILE = 256
EXPERT_CHUNK = 512
VMEM_LIMIT = 56 * 1024 * 1024


def _rms(x, g):
    ms = jnp.mean(x * x, axis=-1, keepdims=True)
    return x * lax.rsqrt(ms + EPS) * g


def _layernorm(x, g, b):
    mu = jnp.mean(x, axis=-1, keepdims=True)
    xc = x - mu
    var = jnp.mean(xc * xc, axis=-1, keepdims=True)
    return xc * lax.rsqrt(var + EPS) * g + b


def _sigmoid(x):
    return 1.0 / (1.0 + jnp.exp(-x))


def _const_spec(shape):
    return pl.BlockSpec(shape, lambda *_: (0,) * len(shape), pipeline_mode=pl.Buffered(1))


def _mixer_body(x_ref, ng_ref, win_ref, cw_ref, cb_ref, clg_ref, clb_ref, cwo_ref,
                pw_ref, ps_ref, slg_ref, slb_ref, sws_ref, sbs_ref, swo_ref, wo_ref,
                o_ref, abuf, ash, cbuf, pbuf, tbuf, usbuf):
    ts = SEQ_TILE
    s = pl.program_id(1)

    @pl.when(s == 0)
    def _():
        abuf[0:HALO, :] = jnp.zeros((HALO, D_CONV), F32)
        pbuf[0:HALO, :] = jnp.zeros((HALO, D_POOL), F32)

    x = x_ref[0]
    h = _rms(x, ng_ref[...]).astype(BF16)

    def proj(lo, hi):
        return jnp.dot(h, win_ref[:, lo:hi], preferred_element_type=F32)

    ain = proj(0, SPLIT_A)
    abuf[HALO:HALO + ts, :] = ain[:, :D_CONV] * _sigmoid(ain[:, D_CONV:])
    for r in range(1, SUBLANES):
        ash[r - 1] = abuf[r:r + ts + HALO - SUBLANES, :]

    def conv_chunk(i, carry):
        r0 = pl.multiple_of(i * CONV_ROWS, CONV_ROWS)
        acc = jnp.broadcast_to(cb_ref[...], (CONV_ROWS, D_CONV))
        for k in range(CONV_WIDTH):
            q, r = divmod(HALO - (CONV_WIDTH - 1) + k, SUBLANES)
            start = pl.multiple_of(r0 + SUBLANES * q, SUBLANES)
            if r == 0:
                win = abuf[pl.ds(start, CONV_ROWS), :]
            else:
                win = ash[r - 1, pl.ds(start, CONV_ROWS), :]
            acc = acc + win * cw_ref[k:k + 1, :]
        cbuf[pl.ds(r0, CONV_ROWS), :] = acc
        return carry

    lax.fori_loop(0, ts // CONV_ROWS, conv_chunk, 0)
    abuf[0:HALO, :] = abuf[ts:ts + HALO, :]

    ca = _layernorm(cbuf[...], clg_ref[...], clb_ref[...])
    ca = (ca * _sigmoid(ca)).astype(BF16)
    y_a = jnp.dot(ca, cwo_ref[...], preferred_element_type=F32)
    mixed = _sigmoid(proj(GATE_A, GATE_B)) * y_a

    p = proj(SPLIT_A, SPLIT_B)
    pbuf[HALO:HALO + ts, :] = p
    tbuf[0:SUBLANES, :] = jnp.zeros((SUBLANES, POOL_GROUP), F32)
    pos1 = (s * ts + 1 + lax.broadcasted_iota(jnp.int32, (ts, 1), 0))
    ext = ts + HALO
    yb_parts = []
    for g, w in enumerate(POOL_WINDOWS):
        lo = g * POOL_GROUP
        cur = pbuf[:, lo:lo + POOL_GROUP]
        sh = 1
        while sh < w:
            tbuf[SUBLANES:SUBLANES + ext, :] = cur
            cur = cur + tbuf[SUBLANES - sh:SUBLANES - sh + ext, :]
            sh *= 2
        cnt = jnp.minimum(pos1, w).astype(F32)
        pooled = cur[HALO:, :] / cnt - p[:, lo:lo + POOL_GROUP]
        yb_parts.append(jnp.dot(pooled.astype(BF16), pw_ref[g], preferred_element_type=F32))
    pbuf[0:HALO, :] = pbuf[ts:ts + HALO, :]
    y_b = jnp.concatenate(yb_parts, axis=-1) * ps_ref[...]
    mixed = mixed + _sigmoid(proj(GATE_B, GATE_C)) * y_b

    u = proj(SPLIT_B, SPLIT_U)
    vn = _layernorm(proj(SPLIT_U, SPLIT_V), slg_ref[...], slb_ref[...]).astype(BF16)
    row = lax.broadcasted_iota(jnp.int32, (CHUNK, CHUNK), 0)
    col = lax.broadcasted_iota(jnp.int32, (CHUNK, CHUNK), 1)
    for hh in range(SGU_HEADS):
        wm = jnp.where(col <= row, sws_ref[hh], 0.0).astype(BF16)
        c0 = hh * SGU_HEAD_DIM
        for c in range(ts // CHUNK):
            r0 = c * CHUNK
            sp = jnp.dot(wm, vn[r0:r0 + CHUNK, c0:c0 + SGU_HEAD_DIM], preferred_element_type=F32)
            sp = sp + sbs_ref[hh]
            usbuf[r0:r0 + CHUNK, c0:c0 + SGU_HEAD_DIM] = (
                u[r0:r0 + CHUNK, c0:c0 + SGU_HEAD_DIM] * sp).astype(BF16)
    y_c = jnp.dot(usbuf[...], swo_ref[...], preferred_element_type=F32)
    mixed = mixed + _sigmoid(proj(GATE_C, D_IN)) * y_c

    o_ref[0] = x + jnp.dot(mixed.astype(BF16), wo_ref[...], preferred_element_type=F32)


def _mixer_layer(x, ng, w_in, cw, cb, clg, clb, cwo, pw, ps, slg, slb, sws, sbs, swo, wo):
    b, s, d = x.shape
    ts = SEQ_TILE
    row = lambda v: v.reshape(1, -1)
    sbs_b = jnp.broadcast_to(sbs[:, :, None], (SGU_HEADS, CHUNK, SGU_HEAD_DIM))
    args = (x, row(ng), w_in.astype(BF16), cw, row(cb), row(clg), row(clb), cwo.astype(BF16),
            pw.astype(BF16), row(ps), row(slg), row(slb), sws, sbs_b, swo.astype(BF16), wo.astype(BF16))
    x_spec = pl.BlockSpec((1, ts, d), lambda bi, si: (bi, si, 0))
    in_specs = [x_spec] + [_const_spec(a.shape) for a in args[1:]]
    return pl.pallas_call(
        _mixer_body,
        grid=(b, s // ts),
        in_specs=in_specs,
        out_specs=x_spec,
        out_shape=jax.ShapeDtypeStruct(x.shape, F32),
        scratch_shapes=[
            pltpu.VMEM((ts + HALO, D_CONV), F32),
            pltpu.VMEM((SUBLANES - 1, ts + HALO - SUBLANES, D_CONV), F32),
            pltpu.VMEM((ts, D_CONV), F32),
            pltpu.VMEM((ts + HALO, D_POOL), F32),
            pltpu.VMEM((SUBLANES + ts + HALO, POOL_GROUP), F32),
            pltpu.VMEM((ts, D_SGU), BF16),
        ],
        compiler_params=pltpu.CompilerParams(
            dimension_semantics=("arbitrary", "arbitrary"), vmem_limit_bytes=VMEM_LIMIT),
        name="token_mixer",
    )(*args)


def _dense_ffn_body(x_ref, g_ref, w1_ref, w3_ref, w2_ref, o_ref):
    x = x_ref[...]
    h = _rms(x, g_ref[...]).astype(BF16)
    a = jnp.dot(h, w1_ref[...], preferred_element_type=F32)
    b = jnp.dot(h, w3_ref[...], preferred_element_type=F32)
    act = (a * _sigmoid(a) * b).astype(BF16)
    o_ref[...] = x + jnp.dot(act, w2_ref[...], preferred_element_type=F32)


def _dense_ffn(x2, g, w1, w3, w2):
    t, d = x2.shape
    tm = FFN_TILE
    args = (x2, g.reshape(1, -1), w1.astype(BF16), w3.astype(BF16), w2.astype(BF16))
    x_spec = pl.BlockSpec((tm, d), lambda i: (i, 0))
    return pl.pallas_call(
        _dense_ffn_body,
        grid=(t // tm,),
        in_specs=[x_spec] + [_const_spec(a.shape) for a in args[1:]],
        out_specs=x_spec,
        out_shape=jax.ShapeDtypeStruct(x2.shape, F32),
        compiler_params=pltpu.CompilerParams(
            dimension_semantics=("arbitrary",), vmem_limit_bytes=VMEM_LIMIT),
        name="dense_ffn",
    )(*args)


def _router_body(x_ref, g_ref, wr_ref, idx_ref, wt_ref):
    h = _rms(x_ref[...], g_ref[...])
    logits = lax.dot_general(wr_ref[...], h, (((1,), (1,)), ((), ())),
                             precision=lax.Precision.HIGHEST, preferred_element_type=F32)
    eid = lax.broadcasted_iota(jnp.int32, logits.shape, 0)
    m1 = jnp.max(logits, axis=0, keepdims=True)
    i1 = jnp.min(jnp.where(logits == m1, eid, N_EXPERTS), axis=0, keepdims=True)
    rest = jnp.where(eid == i1, -jnp.inf, logits)
    m2 = jnp.max(rest, axis=0, keepdims=True)
    i2 = jnp.min(jnp.where(rest == m2, eid, N_EXPERTS), axis=0, keepdims=True)
    e2 = jnp.exp(m2 - m1)
    den = 1.0 + e2
    idx_ref[...] = jnp.concatenate([i1, i2], axis=0)
    wt_ref[...] = jnp.concatenate([1.0 / den, e2 / den], axis=0)


def _router(x2, g, w_router):
    t, d = x2.shape
    tm = ROUTER_TILE
    out_spec = pl.BlockSpec((TOP_K, tm), lambda i: (0, i))
    return pl.pallas_call(
        _router_body,
        grid=(t // tm,),
        in_specs=[pl.BlockSpec((tm, d), lambda i: (i, 0)),
                  _const_spec((1, d)), _const_spec((N_EXPERTS, d))],
        out_specs=[out_spec, out_spec],
        out_shape=[jax.ShapeDtypeStruct((TOP_K, t), jnp.int32),
                   jax.ShapeDtypeStruct((TOP_K, t), F32)],
        compiler_params=pltpu.CompilerParams(dimension_semantics=("arbitrary",)),
        name="moe_router",
    )(x2, g.reshape(1, -1), w_router.T)


def _expert_body(be_ref, nu_ref, rt_hbm, x_hbm, g_ref, w1_ref, w3_ref, w2_ref, o_ref,
                 idx_smem, xbuf, actbuf, isem, gsem):
    rb = MOE_BLOCK
    i = pl.program_id(0)
    nused = nu_ref[0]
    slot = lax.rem(i, 2)

    def idx_copy(blk, sl):
        return pltpu.make_async_copy(rt_hbm.at[blk], idx_smem.at[sl], isem.at[sl])

    def issue_gather(sl):
        def body(r, carry):
            tok = idx_smem[sl, r]
            pltpu.make_async_copy(x_hbm.at[pl.ds(tok, 1)], xbuf.at[sl, pl.ds(r, 1)],
                                  gsem.at[sl]).start()
            return carry
        lax.fori_loop(0, rb, body, 0)

    def wait_gather(sl):
        pltpu.make_async_copy(x_hbm.at[pl.ds(0, rb)], xbuf.at[sl], gsem.at[sl]).wait()

    @pl.when(i == 0)
    def _():
        c = idx_copy(0, 0)
        c.start()
        c.wait()
        issue_gather(0)

        @pl.when(nused > 1)
        def _():
            idx_copy(1, 1).start()

    @pl.when(i + 1 < nused)
    def _():
        idx_copy(i + 1, 1 - slot).wait()
        issue_gather(1 - slot)

        @pl.when(i + 2 < nused)
        def _():
            idx_copy(i + 2, slot).start()

    @pl.when(i < nused)
    def _():
        wait_gather(slot)
        h = _rms(xbuf[slot], g_ref[...]).astype(BF16)
        for j in range(D_EXPERT // EXPERT_CHUNK):
            cs = slice(j * EXPERT_CHUNK, (j + 1) * EXPERT_CHUNK)
            a = jnp.dot(h, w1_ref[0, :, cs], preferred_element_type=F32)
            b = jnp.dot(h, w3_ref[0, :, cs], preferred_element_type=F32)
            actbuf[:, cs] = (a * _sigmoid(a) * b).astype(BF16)
        o_ref[...] = jnp.dot(actbuf[...], w2_ref[0], preferred_element_type=F32)

    @pl.when(i >= nused)
    def _():
        o_ref[...] = jnp.zeros(o_ref.shape, F32)


def _expert_ffn(block_e, nused, row_tok, x2, g, w1, w3, w2):
    t, d = x2.shape
    nb, rb = row_tok.shape
    w13_spec = pl.BlockSpec((1, d, D_EXPERT), lambda i, be, nu: (be[i], 0, 0),
                            pipeline_mode=pl.Buffered(1))
    w2_spec = pl.BlockSpec((1, D_EXPERT, d), lambda i, be, nu: (be[i], 0, 0),
                           pipeline_mode=pl.Buffered(1))
    grid_spec = pltpu.PrefetchScalarGridSpec(
        num_scalar_prefetch=2,
        grid=(nb,),
        in_specs=[pl.BlockSpec(memory_space=pl.ANY),
                  pl.BlockSpec(memory_space=pl.ANY),
                  pl.BlockSpec((1, d), lambda i, be, nu: (0, 0), pipeline_mode=pl.Buffered(1)),
                  w13_spec, w13_spec, w2_spec],
        out_specs=pl.BlockSpec((rb, d), lambda i, be, nu: (i, 0)),
        scratch_shapes=[
            pltpu.SMEM((2, rb), jnp.int32),
            pltpu.VMEM((2, rb, d), F32),
            pltpu.VMEM((rb, D_EXPERT), BF16),
            pltpu.SemaphoreType.DMA((2,)),
            pltpu.SemaphoreType.DMA((2,)),
        ],
    )
    return pl.pallas_call(
        _expert_body,
        grid_spec=grid_spec,
        out_shape=jax.ShapeDtypeStruct((nb * rb, d), F32),
        compiler_params=pltpu.CompilerParams(
            dimension_semantics=("arbitrary",), vmem_limit_bytes=VMEM_LIMIT),
        name="moe_experts",
    )(block_e, nused, row_tok, x2, g.reshape(1, -1), w1.astype(BF16), w3.astype(BF16), w2.astype(BF16))


def _combine_body(pos_hbm, y_hbm, x_ref, wt_ref, g_ref, o_ref, idx_smem, ybuf, isem, gsem):
    tm = COMBINE_TILE
    nrow = TOP_K * tm
    i = pl.program_id(0)
    n = pl.num_programs(0)
    slot = lax.rem(i, 2)

    def idx_copy(blk, sl):
        return pltpu.make_async_copy(pos_hbm.at[blk], idx_smem.at[sl], isem.at[sl])

    def issue_gather(sl):
        def body(r, carry):
            src = idx_smem[sl, r]
            pltpu.make_async_copy(y_hbm.at[pl.ds(src, 1)], ybuf.at[sl, pl.ds(r, 1)],
                                  gsem.at[sl]).start()
            return carry
        lax.fori_loop(0, nrow, body, 0)

    def wait_gather(sl):
        pltpu.make_async_copy(y_hbm.at[pl.ds(0, nrow)], ybuf.at[sl], gsem.at[sl]).wait()

    @pl.when(i == 0)
    def _():
        c = idx_copy(0, 0)
        c.start()
        c.wait()
        issue_gather(0)

        @pl.when(n > 1)
        def _():
            idx_copy(1, 1).start()

    @pl.when(i + 1 < n)
    def _():
        idx_copy(i + 1, 1 - slot).wait()
        issue_gather(1 - slot)

        @pl.when(i + 2 < n)
        def _():
            idx_copy(i + 2, slot).start()

    wait_gather(slot)
    moe = wt_ref[:, 0:1] * ybuf[slot, 0:tm, :] + wt_ref[:, 1:2] * ybuf[slot, tm:nrow, :]
    o_ref[...] = _rms(x_ref[...] + moe, g_ref[...])


def _combine(pos_tiles, y, x2, wt, g):
    t, d = x2.shape
    tm = COMBINE_TILE
    nrow = TOP_K * tm
    return pl.pallas_call(
        _combine_body,
        grid=(t // tm,),
        in_specs=[pl.BlockSpec(memory_space=pl.ANY),
                  pl.BlockSpec(memory_space=pl.ANY),
                  pl.BlockSpec((tm, d), lambda i: (i, 0)),
                  pl.BlockSpec((tm, TOP_K), lambda i: (i, 0)),
                  _const_spec((1, d))],
        out_specs=pl.BlockSpec((tm, d), lambda i: (i, 0)),
        out_shape=jax.ShapeDtypeStruct(x2.shape, F32),
        scratch_shapes=[
            pltpu.SMEM((2, nrow), jnp.int32),
            pltpu.VMEM((2, nrow, d), F32),
            pltpu.SemaphoreType.DMA((2,)),
            pltpu.SemaphoreType.DMA((2,)),
        ],
        compiler_params=pltpu.CompilerParams(dimension_semantics=("arbitrary",)),
        name="moe_combine_norm",
    )(pos_tiles, y, x2, wt, g.reshape(1, -1))


def _group_rows(idx):
    t = idx.shape[1]
    n_assign = TOP_K * t
    n_rows = -(-n_assign // MOE_BLOCK) * MOE_BLOCK + N_EXPERTS * MOE_BLOCK
    n_blocks = n_rows // MOE_BLOCK
    e_flat = idx.reshape(-1)
    onehot = (e_flat[:, None] == jnp.arange(N_EXPERTS, dtype=jnp.int32)[None, :]).astype(jnp.int32)
    csum = jnp.cumsum(onehot, axis=0)
    counts = csum[-1]
    rank = jnp.take_along_axis(csum, e_flat[:, None], axis=1)[:, 0] - 1
    padded = (counts + MOE_BLOCK - 1) // MOE_BLOCK * MOE_BLOCK
    pend = jnp.cumsum(padded)
    pstart = pend - padded
    dest = pstart[e_flat] + rank
    tok_flat = jnp.tile(jnp.arange(t, dtype=jnp.int32), TOP_K)
    row_tok = jnp.zeros((n_rows,), jnp.int32).at[dest].set(tok_flat)
    block_start = jnp.arange(n_blocks, dtype=jnp.int32) * MOE_BLOCK
    block_e = jnp.minimum(jnp.searchsorted(pend, block_start, side='right'),
                          N_EXPERTS - 1).astype(jnp.int32)
    nused = (pend[-1] // MOE_BLOCK).astype(jnp.int32).reshape(1)
    return row_tok.reshape(n_blocks, MOE_BLOCK), block_e, nused, dest.reshape(TOP_K, t)


def _moe_and_final_norm(x2, ffn_g, w_router, w1, w3, w2, final_g):
    t, d = x2.shape
    idx, wts = _router(x2, ffn_g, w_router)
    row_tok, block_e, nused, pos = _group_rows(idx)
    y = _expert_ffn(block_e, nused, row_tok, x2, ffn_g, w1, w3, w2)
    tm = COMBINE_TILE
    pos_tiles = pos.reshape(TOP_K, t // tm, tm).transpose(1, 0, 2).reshape(t // tm, TOP_K * tm)
    return _combine(pos_tiles, y, x2, wts.T, final_g)


def kernel(x, mix_norm_g, w_in, conv_w, conv_b, conv_ln_g, conv_ln_b, conv_w_out, pool_w, pool_scale,
           sgu_ln_g, sgu_ln_b, sgu_w_s, sgu_b_s, sgu_w_out, w_out, ffn_norm_g, dense_w1, dense_w3,
           dense_w2, router_w, expert_w1, expert_w3, expert_w2, final_norm_g):
    b, s, d = x.shape

    def mixer(xx, l):
        return _mixer_layer(xx, mix_norm_g[l], w_in[l], conv_w[l], conv_b[l], conv_ln_g[l], conv_ln_b[l],
                            conv_w_out[l], pool_w[l], pool_scale[l], sgu_ln_g[l], sgu_ln_b[l],
                            sgu_w_s[l], sgu_b_s[l], sgu_w_out[l], w_out[l])

    x = mixer(x, 0)
    x2 = _dense_ffn(x.reshape(b * s, d), ffn_norm_g[0], dense_w1[0], dense_w3[0], dense_w2[0])
    x = mixer(x2.reshape(b, s, d), 1)
    out = _moe_and_final_norm(x.reshape(b * s, d), ffn_norm_g[1], router_w[0], expert_w1[0],
                              expert_w3[0], expert_w2[0], final_norm_g)
    return out.reshape(b, s, d)
```

```python
import jax
import jax.numpy as jnp
from jax import lax
from jax.experimental import pallas as pl
from jax.experimental.pallas import tpu as pltpu

F32 = jnp.float32
BF16 = jnp.bfloat16

D_MODEL = 1024
DEPTH = 2
D_CONV = 512
CONV_WIDTH = 31
D_POOL = 512
POOL_WINDOWS = (2, 4, 8, 16)
N_POOL_GROUPS = 4
POOL_GROUP = D_POOL // N_POOL_GROUPS
POOL_OUT_GROUP = D_MODEL // N_POOL_GROUPS
D_SGU = 512
SGU_HEADS = 4
SGU_HEAD_DIM = D_SGU // SGU_HEADS
CHUNK = 128
SPLIT_A = 2 * D_CONV
SPLIT_B = SPLIT_A + D_POOL
SPLIT_U = SPLIT_B + D_SGU
SPLIT_V = SPLIT_U + D_SGU
GATE_A = SPLIT_V
GATE_B = GATE_A + D_MODEL
GATE_C = GATE_B + D_MODEL
D_IN = GATE_C + D_MODEL
D_FF = 2816
N_EXPERTS = 8
TOP_K = 2
D_EXPERT = 3584
MOE_BLOCK = 512
EPS = 1e-6

SUBLANES = 8
LANES = 128
LANE_TILES = D_MODEL // LANES
HALO = 32
SEQ_TILE = 512
CONV_ROWS = 32
FFN_TILE = 512
MOE_TILE = 512
EXPERT_CHUNK = 512
VMEM_LIMIT = 56 * 1024 * 1024

GROUP_SUBLANES = LANE_TILES * SUBLANES
TILE_GROUPS = MOE_TILE // SUBLANES
BLOCK_GROUPS = MOE_BLOCK // SUBLANES
LOCAL_GROUPS = TOP_K * TILE_GROUPS + N_EXPERTS
SEG_BITS = TILE_GROUPS.bit_length()


def _rms(x, g):
    ms = jnp.mean(x * x, axis=-1, keepdims=True)
    return x * lax.rsqrt(ms + EPS) * g


def _layernorm(x, g, b):
    mu = jnp.mean(x, axis=-1, keepdims=True)
    xc = x - mu
    var = jnp.mean(xc * xc, axis=-1, keepdims=True)
    return xc * lax.rsqrt(var + EPS) * g + b


def _sigmoid(x):
    return 1.0 / (1.0 + jnp.exp(-x))


def _const_spec(shape):
    return pl.BlockSpec(shape, lambda *_: (0,) * len(shape), pipeline_mode=pl.Buffered(1))


def _to_groups(v):
    rows = v.shape[0]
    return [v[:, j * LANES:(j + 1) * LANES].reshape(rows // SUBLANES, SUBLANES, LANES)
            for j in range(LANE_TILES)]


def _store_groups(ref, v):
    for j, piece in enumerate(_to_groups(v)):
        ref[:, j * SUBLANES:(j + 1) * SUBLANES, :] = piece


def _load_groups(ref):
    rows = ref.shape[0] * SUBLANES
    return jnp.concatenate(
        [ref[:, j * SUBLANES:(j + 1) * SUBLANES, :].reshape(rows, LANES) for j in range(LANE_TILES)],
        axis=-1)


def _row_slice(row):
    return (lax.shift_right_logical(row, 3), pl.ds(lax.bitwise_and(row, SUBLANES - 1), LANE_TILES,
                                                    stride=SUBLANES))


def _mixer_body(x_ref, ng_ref, win_ref, cw_ref, cb_ref, clg_ref, clb_ref, cwo_ref,
                pw_ref, ps_ref, slg_ref, slb_ref, sws_ref, sbs_ref, swo_ref, wo_ref,
                o_ref, abuf, ash, cbuf, pbuf, tbuf, usbuf):
    ts = SEQ_TILE
    s = pl.program_id(1)

    @pl.when(s == 0)
    def _():
        abuf[0:HALO, :] = jnp.zeros((HALO, D_CONV), F32)
        pbuf[0:HALO, :] = jnp.zeros((HALO, D_POOL), F32)

    x = x_ref[0]
    h = _rms(x, ng_ref[...]).astype(BF16)

    def proj(lo, hi):
        return jnp.dot(h, win_ref[:, lo:hi], preferred_element_type=F32)

    ain = proj(0, SPLIT_A)
    abuf[HALO:HALO + ts, :] = ain[:, :D_CONV] * _sigmoid(ain[:, D_CONV:])
    for r in range(1, SUBLANES):
        ash[r - 1] = abuf[r:r + ts + HALO - SUBLANES, :]

    def conv_chunk(i, carry):
        r0 = pl.multiple_of(i * CONV_ROWS, CONV_ROWS)
        acc = jnp.broadcast_to(cb_ref[...], (CONV_ROWS, D_CONV))
        for k in range(CONV_WIDTH):
            q, r = divmod(HALO - (CONV_WIDTH - 1) + k, SUBLANES)
            start = pl.multiple_of(r0 + SUBLANES * q, SUBLANES)
            if r == 0:
                win = abuf[pl.ds(start, CONV_ROWS), :]
            else:
                win = ash[r - 1, pl.ds(start, CONV_ROWS), :]
            acc = acc + win * cw_ref[k:k + 1, :]
        cbuf[pl.ds(r0, CONV_ROWS), :] = acc
        return carry

    lax.fori_loop(0, ts // CONV_ROWS, conv_chunk, 0)
    abuf[0:HALO, :] = abuf[ts:ts + HALO, :]

    ca = _layernorm(cbuf[...], clg_ref[...], clb_ref[...])
    ca = (ca * _sigmoid(ca)).astype(BF16)
    y_a = jnp.dot(ca, cwo_ref[...], preferred_element_type=F32)
    mixed = _sigmoid(proj(GATE_A, GATE_B)) * y_a

    p = proj(SPLIT_A, SPLIT_B)
    pbuf[HALO:HALO + ts, :] = p
    tbuf[0:SUBLANES, :] = jnp.zeros((SUBLANES, POOL_GROUP), F32)
    pos1 = (s * ts + 1 + lax.broadcasted_iota(jnp.int32, (ts, 1), 0))
    ext = ts + HALO
    yb_parts = []
    for g, w in enumerate(POOL_WINDOWS):
        lo = g * POOL_GROUP
        cur = pbuf[:, lo:lo + POOL_GROUP]
        sh = 1
        while sh < w:
            tbuf[SUBLANES:SUBLANES + ext, :] = cur
            cur = cur + tbuf[SUBLANES - sh:SUBLANES - sh + ext, :]
            sh *= 2
        cnt = jnp.minimum(pos1, w).astype(F32)
        pooled = cur[HALO:, :] / cnt - p[:, lo:lo + POOL_GROUP]
        yb_parts.append(jnp.dot(pooled.astype(BF16), pw_ref[g], preferred_element_type=F32))
    pbuf[0:HALO, :] = pbuf[ts:ts + HALO, :]
    y_b = jnp.concatenate(yb_parts, axis=-1) * ps_ref[...]
    mixed = mixed + _sigmoid(proj(GATE_B, GATE_C)) * y_b

    u = proj(SPLIT_B, SPLIT_U)
    vn = _layernorm(proj(SPLIT_U, SPLIT_V), slg_ref[...], slb_ref[...]).astype(BF16)
    row = lax.broadcasted_iota(jnp.int32, (CHUNK, CHUNK), 0)
    col = lax.broadcasted_iota(jnp.int32, (CHUNK, CHUNK), 1)
    for hh in range(SGU_HEADS):
        wm = jnp.where(col <= row, sws_ref[hh], 0.0).astype(BF16)
        c0 = hh * SGU_HEAD_DIM
        for c in range(ts // CHUNK):
            r0 = c * CHUNK
            sp = jnp.dot(wm, vn[r0:r0 + CHUNK, c0:c0 + SGU_HEAD_DIM], preferred_element_type=F32)
            sp = sp + sbs_ref[hh]
            usbuf[r0:r0 + CHUNK, c0:c0 + SGU_HEAD_DIM] = (
                u[r0:r0 + CHUNK, c0:c0 + SGU_HEAD_DIM] * sp).astype(BF16)
    y_c = jnp.dot(usbuf[...], swo_ref[...], preferred_element_type=F32)
    mixed = mixed + _sigmoid(proj(GATE_C, D_IN)) * y_c

    o_ref[0] = x + jnp.dot(mixed.astype(BF16), wo_ref[...], preferred_element_type=F32)


def _mixer_layer(x, ng, w_in, cw, cb, clg, clb, cwo, pw, ps, slg, slb, sws, sbs, swo, wo):
    b, s, d = x.shape
    ts = SEQ_TILE
    row = lambda v: v.reshape(1, -1)
    sbs_b = jnp.broadcast_to(sbs[:, :, None], (SGU_HEADS, CHUNK, SGU_HEAD_DIM))
    args = (x, row(ng), w_in.astype(BF16), cw, row(cb), row(clg), row(clb), cwo.astype(BF16),
            pw.astype(BF16), row(ps), row(slg), row(slb), sws, sbs_b, swo.astype(BF16), wo.astype(BF16))
    x_spec = pl.BlockSpec((1, ts, d), lambda bi, si: (bi, si, 0))
    in_specs = [x_spec] + [_const_spec(a.shape) for a in args[1:]]
    return pl.pallas_call(
        _mixer_body,
        grid=(b, s // ts),
        in_specs=in_specs,
        out_specs=x_spec,
        out_shape=jax.ShapeDtypeStruct(x.shape, F32),
        scratch_shapes=[
            pltpu.VMEM((ts + HALO, D_CONV), F32),
            pltpu.VMEM((SUBLANES - 1, ts + HALO - SUBLANES, D_CONV), F32),
            pltpu.VMEM((ts, D_CONV), F32),
            pltpu.VMEM((ts + HALO, D_POOL), F32),
            pltpu.VMEM((SUBLANES + ts + HALO, POOL_GROUP), F32),
            pltpu.VMEM((ts, D_SGU), BF16),
        ],
        compiler_params=pltpu.CompilerParams(
            dimension_semantics=("arbitrary", "arbitrary"), vmem_limit_bytes=VMEM_LIMIT),
        name="token_mixer",
    )(*args)


def _dense_ffn_body(x_ref, g_ref, w1_ref, w3_ref, w2_ref, o_ref):
    x = x_ref[...]
    h = _rms(x, g_ref[...]).astype(BF16)
    a = jnp.dot(h, w1_ref[...], preferred_element_type=F32)
    b = jnp.dot(h, w3_ref[...], preferred_element_type=F32)
    act = (a * _sigmoid(a) * b).astype(BF16)
    o_ref[...] = x + jnp.dot(act, w2_ref[...], preferred_element_type=F32)


def _dense_ffn(x2, g, w1, w3, w2):
    t, d = x2.shape
    tm = FFN_TILE
    args = (x2, g.reshape(1, -1), w1.astype(BF16), w3.astype(BF16), w2.astype(BF16))
    x_spec = pl.BlockSpec((tm, d), lambda i: (i, 0))
    return pl.pallas_call(
        _dense_ffn_body,
        grid=(t // tm,),
        in_specs=[x_spec] + [_const_spec(a.shape) for a in args[1:]],
        out_specs=x_spec,
        out_shape=jax.ShapeDtypeStruct(x2.shape, F32),
        compiler_params=pltpu.CompilerParams(
            dimension_semantics=("arbitrary",), vmem_limit_bytes=VMEM_LIMIT),
        name="dense_ffn",
    )(*args)


def _router_body(x_ref, g_ref, wr_ref, tri_ref, q_ref, wt_ref, cnt_ref):
    tm = MOE_TILE
    h = _rms(x_ref[...], g_ref[...])
    logits = lax.dot_general(wr_ref[...], h, (((1,), (1,)), ((), ())),
                             precision=lax.Precision.HIGHEST, preferred_element_type=F32)
    eid = lax.broadcasted_iota(jnp.int32, logits.shape, 0)
    m1 = jnp.max(logits, axis=0, keepdims=True)
    i1 = jnp.min(jnp.where(logits == m1, eid, N_EXPERTS), axis=0, keepdims=True)
    rest = jnp.where(eid == i1, -jnp.inf, logits)
    m2 = jnp.max(rest, axis=0, keepdims=True)
    i2 = jnp.min(jnp.where(rest == m2, eid, N_EXPERTS), axis=0, keepdims=True)
    e2 = jnp.exp(m2 - m1)
    den = 1.0 + e2
    wt_ref[0] = jnp.concatenate([1.0 / den, e2 / den], axis=0)

    oh1 = eid == i1
    oh2 = eid == i2
    onehot = jnp.where(oh1 | oh2, 1.0, 0.0)
    incl = jnp.dot(onehot.astype(BF16), tri_ref[...], preferred_element_type=F32)
    cnt = jnp.sum(onehot, axis=1, keepdims=True).astype(jnp.int32)
    seg = (lax.shift_right_logical(cnt + (SUBLANES - 1), 3) * SUBLANES).astype(F32)
    erow = lax.broadcasted_iota(jnp.int32, (N_EXPERTS, 1), 0)
    seg_start = jnp.zeros((N_EXPERTS, 1), F32)
    for e in range(N_EXPERTS - 1):
        seg_start = seg_start + jnp.where(erow > e, seg[e:e + 1, :], 0.0)
    base = seg_start + (incl - onehot)
    q1 = jnp.sum(jnp.where(oh1, base, 0.0), axis=0, keepdims=True)
    q2 = jnp.sum(jnp.where(oh2, base, 0.0), axis=0, keepdims=True)
    q_ref[0] = jnp.concatenate([q1, q2], axis=0).astype(jnp.int32)
    cnt_ref[0] = jnp.broadcast_to(cnt, (N_EXPERTS, LANES))


def _router(x2, g, w_router):
    t, d = x2.shape
    tm = MOE_TILE
    nt = t // tm
    tri = jnp.triu(jnp.ones((tm, tm), BF16))
    tile_spec = pl.BlockSpec((1, TOP_K, tm), lambda i: (i, 0, 0))
    return pl.pallas_call(
        _router_body,
        grid=(nt,),
        in_specs=[pl.BlockSpec((tm, d), lambda i: (i, 0)),
                  _const_spec((1, d)), _const_spec((N_EXPERTS, d)), _const_spec((tm, tm))],
        out_specs=[tile_spec, tile_spec, pl.BlockSpec((1, N_EXPERTS, LANES), lambda i: (i, 0, 0))],
        out_shape=[jax.ShapeDtypeStruct((nt, TOP_K, tm), jnp.int32),
                   jax.ShapeDtypeStruct((nt, TOP_K, tm), F32),
                   jax.ShapeDtypeStruct((nt, N_EXPERTS, LANES), jnp.int32)],
        compiler_params=pltpu.CompilerParams(dimension_semantics=("arbitrary",)),
        name="moe_router",
    )(x2, g.reshape(1, -1), w_router.T, tri)


def _segment_tables(cnt):
    seg = (cnt + (SUBLANES - 1)) // SUBLANES
    local_start = jnp.cumsum(seg, axis=1) - seg
    total = jnp.sum(seg, axis=0)
    padded = (total + BLOCK_GROUPS - 1) // BLOCK_GROUPS * BLOCK_GROUPS
    pend = jnp.cumsum(padded)
    pstart = pend - padded
    global_start = pstart[None, :] + jnp.cumsum(seg, axis=0) - seg
    nused = (pend[-1] // BLOCK_GROUPS).astype(jnp.int32).reshape(1)
    return dict(
        nseg=seg.reshape(-1).astype(jnp.int32),
        local_start=local_start.reshape(-1).astype(jnp.int32),
        global_start=global_start.reshape(-1).astype(jnp.int32),
        tail_start=(pstart + total).astype(jnp.int32),
        tail_len=(padded - total).astype(jnp.int32),
        pend=pend.astype(jnp.int32),
        nused=nused)


def _segment_pieces(nseg, local_start, global_start, tile, make_copy, act):
    for e in range(N_EXPERTS):
        k = tile * N_EXPERTS + e
        ng = nseg[k]
        l0 = local_start[k]
        g0 = global_start[k]
        for b in reversed(range(SEG_BITS)):
            size = 1 << b
            off = lax.shift_left(lax.shift_right_logical(ng, b + 1), b + 1)

            @pl.when(lax.bitwise_and(lax.shift_right_logical(ng, b), 1) == 1)
            def _():
                act(make_copy(l0 + off, g0 + off, size))


def _dispatch_body(ns_ref, ls_ref, gs_ref, ts_ref, tl_ref, nu_ref, q_hbm, x_ref, g_ref, xs_hbm,
                   q_smem, z, lbuf, zbuf, qsem, dsem, zsem):
    i = pl.program_id(0)
    n = pl.num_programs(0)
    slot = lax.rem(i, 2)

    def q_copy(tile, sl):
        return pltpu.make_async_copy(q_hbm.at[tile], q_smem.at[sl], qsem.at[sl])

    def seg_copy(sl):
        def make(loc, glob, size):
            return pltpu.make_async_copy(lbuf.at[sl, pl.ds(loc, size)], xs_hbm.at[pl.ds(glob, size)],
                                         dsem.at[sl])
        return make

    @pl.when(i == 0)
    def _():
        q_copy(0, 0).start()
        lbuf[...] = jnp.zeros(lbuf.shape, F32)

    q_copy(i, slot).wait()

    @pl.when(i + 1 < n)
    def _():
        q_copy(i + 1, 1 - slot).start()

    _store_groups(z, _rms(x_ref[...], g_ref[...]))

    def scatter(ii, carry):
        for s in range(SUBLANES):
            t = ii * SUBLANES + s
            row = z[ii, pl.ds(s, LANE_TILES, stride=SUBLANES), :]
            for k in range(TOP_K):
                grp, sub = _row_slice(q_smem[slot, k, t])
                lbuf[slot, grp, sub, :] = row
        return carry

    lax.fori_loop(0, TILE_GROUPS, scatter, 0)

    @pl.when(i > 0)
    def _():
        _segment_pieces(ns_ref, ls_ref, gs_ref, i - 1, seg_copy(1 - slot), lambda c: c.wait())

    _segment_pieces(ns_ref, ls_ref, gs_ref, i, seg_copy(slot), lambda c: c.start())

    @pl.when(i == n - 1)
    def _():
        _segment_pieces(ns_ref, ls_ref, gs_ref, i, seg_copy(slot), lambda c: c.wait())
        zbuf[...] = jnp.zeros(zbuf.shape, F32)

        def tail_pieces(act):
            for e in range(N_EXPERTS):
                ng = tl_ref[e]
                g0 = ts_ref[e]
                for b in reversed(range(SEG_BITS)):
                    size = 1 << b
                    off = lax.shift_left(lax.shift_right_logical(ng, b + 1), b + 1)

                    @pl.when(lax.bitwise_and(lax.shift_right_logical(ng, b), 1) == 1)
                    def _():
                        act(pltpu.make_async_copy(zbuf.at[pl.ds(0, size)],
                                                  xs_hbm.at[pl.ds(g0 + off, size)], zsem.at[0]))

        tail_pieces(lambda c: c.start())
        tail_pieces(lambda c: c.wait())

        def zero_block(blk, carry):
            c = pltpu.make_async_copy(zbuf, xs_hbm.at[pl.ds(blk * BLOCK_GROUPS, BLOCK_GROUPS)], zsem.at[0])
            c.start()
            c.wait()
            return carry

        lax.fori_loop(nu_ref[0], xs_hbm.shape[0] // BLOCK_GROUPS, zero_block, 0)


def _dispatch(tabs, q, x2, g, n_groups):
    t, d = x2.shape
    tm = MOE_TILE
    nt = t // tm
    grid_spec = pltpu.PrefetchScalarGridSpec(
        num_scalar_prefetch=6,
        grid=(nt,),
        in_specs=[pl.BlockSpec(memory_space=pl.ANY),
                  pl.BlockSpec((tm, d), lambda i, *_: (i, 0)),
                  pl.BlockSpec((1, d), lambda i, *_: (0, 0), pipeline_mode=pl.Buffered(1))],
        out_specs=pl.BlockSpec(memory_space=pl.ANY),
        scratch_shapes=[
            pltpu.SMEM((2, TOP_K, tm), jnp.int32),
            pltpu.VMEM((TILE_GROUPS, GROUP_SUBLANES, LANES), F32),
            pltpu.VMEM((2, LOCAL_GROUPS, GROUP_SUBLANES, LANES), F32),
            pltpu.VMEM((BLOCK_GROUPS, GROUP_SUBLANES, LANES), F32),
            pltpu.SemaphoreType.DMA((2,)),
            pltpu.SemaphoreType.DMA((2,)),
            pltpu.SemaphoreType.DMA((1,)),
        ],
    )
    return pl.pallas_call(
        _dispatch_body,
        grid_spec=grid_spec,
        out_shape=jax.ShapeDtypeStruct((n_groups, GROUP_SUBLANES, LANES), F32),
        compiler_params=pltpu.CompilerParams(
            dimension_semantics=("arbitrary",), vmem_limit_bytes=VMEM_LIMIT),
        name="moe_dispatch",
    )(tabs["nseg"], tabs["local_start"], tabs["global_start"], tabs["tail_start"], tabs["tail_len"],
      tabs["nused"], q, x2, g.reshape(1, -1))


def _expert_body(be_ref, nu_ref, xs_ref, w1_ref, w3_ref, w2_ref, o_ref, actbuf):
    i = pl.program_id(0)
    nused = nu_ref[0]

    @pl.when(i < nused)
    def _():
        h = _load_groups(xs_ref).astype(BF16)
        for j in range(D_EXPERT // EXPERT_CHUNK):
            cs = slice(j * EXPERT_CHUNK, (j + 1) * EXPERT_CHUNK)
            a = jnp.dot(h, w1_ref[0, :, cs], preferred_element_type=F32)
            b = jnp.dot(h, w3_ref[0, :, cs], preferred_element_type=F32)
            actbuf[:, cs] = (a * _sigmoid(a) * b).astype(BF16)
        _store_groups(o_ref, jnp.dot(actbuf[...], w2_ref[0], preferred_element_type=F32))

    @pl.when(i >= nused)
    def _():
        o_ref[...] = jnp.zeros(o_ref.shape, F32)


def _expert_ffn(block_e, nused, xs, w1, w3, w2):
    n_groups = xs.shape[0]
    nb = n_groups // BLOCK_GROUPS
    d = D_MODEL
    w13_spec = pl.BlockSpec((1, d, D_EXPERT), lambda i, be, nu: (be[i], 0, 0),
                            pipeline_mode=pl.Buffered(1))
    w2_spec = pl.BlockSpec((1, D_EXPERT, d), lambda i, be, nu: (be[i], 0, 0),
                           pipeline_mode=pl.Buffered(1))
    blk = (BLOCK_GROUPS, GROUP_SUBLANES, LANES)
    grid_spec = pltpu.PrefetchScalarGridSpec(
        num_scalar_prefetch=2,
        grid=(nb,),
        in_specs=[pl.BlockSpec(blk, lambda i, be, nu: (jnp.minimum(i, nu[0] - 1), 0, 0)),
                  w13_spec, w13_spec, w2_spec],
        out_specs=pl.BlockSpec(blk, lambda i, be, nu: (i, 0, 0)),
        scratch_shapes=[pltpu.VMEM((MOE_BLOCK, D_EXPERT), BF16)],
    )
    return pl.pallas_call(
        _expert_body,
        grid_spec=grid_spec,
        out_shape=jax.ShapeDtypeStruct(xs.shape, F32),
        compiler_params=pltpu.CompilerParams(
            dimension_semantics=("arbitrary",), vmem_limit_bytes=VMEM_LIMIT),
        name="moe_experts",
    )(block_e, nused, xs, w1.astype(BF16), w3.astype(BF16), w2.astype(BF16))


def _combine_body(ns_ref, ls_ref, gs_ref, q_hbm, w_hbm, y_hbm, x_ref, g_ref, o_ref,
                  q_smem, w_smem, ybuf, zo, qsem, wsem, ysem):
    i = pl.program_id(0)
    n = pl.num_programs(0)
    slot = lax.rem(i, 2)

    def meta_copies(tile, sl):
        return (pltpu.make_async_copy(q_hbm.at[tile], q_smem.at[sl], qsem.at[sl]),
                pltpu.make_async_copy(w_hbm.at[tile], w_smem.at[sl], wsem.at[sl]))

    def seg_copy(sl):
        def make(loc, glob, size):
            return pltpu.make_async_copy(y_hbm.at[pl.ds(glob, size)], ybuf.at[sl, pl.ds(loc, size)],
                                         ysem.at[sl])
        return make

    def fetch(tile, sl):
        for c in meta_copies(tile, sl):
            c.start()
        _segment_pieces(ns_ref, ls_ref, gs_ref, tile, seg_copy(sl), lambda c: c.start())

    @pl.when(i == 0)
    def _():
        fetch(0, 0)

    @pl.when(i + 1 < n)
    def _():
        fetch(i + 1, 1 - slot)

    for c in meta_copies(i, slot):
        c.wait()
    _segment_pieces(ns_ref, ls_ref, gs_ref, i, seg_copy(slot), lambda c: c.wait())

    def gather(ii, carry):
        for s in range(SUBLANES):
            t = ii * SUBLANES + s
            g0, s0 = _row_slice(q_smem[slot, 0, t])
            g1, s1 = _row_slice(q_smem[slot, 1, t])
            zo[ii, pl.ds(s, LANE_TILES, stride=SUBLANES), :] = (
                w_smem[slot, 0, t] * ybuf[slot, g0, s0, :] + w_smem[slot, 1, t] * ybuf[slot, g1, s1, :])
        return carry

    lax.fori_loop(0, TILE_GROUPS, gather, 0)
    o_ref[...] = _rms(x_ref[...] + _load_groups(zo), g_ref[...])


def _combine(tabs, q, wts, y, x2, g):
    t, d = x2.shape
    tm = MOE_TILE
    grid_spec = pltpu.PrefetchScalarGridSpec(
        num_scalar_prefetch=3,
        grid=(t // tm,),
        in_specs=[pl.BlockSpec(memory_space=pl.ANY),
                  pl.BlockSpec(memory_space=pl.ANY),
                  pl.BlockSpec(memory_space=pl.ANY),
                  pl.BlockSpec((tm, d), lambda i, *_: (i, 0)),
                  pl.BlockSpec((1, d), lambda i, *_: (0, 0), pipeline_mode=pl.Buffered(1))],
        out_specs=pl.BlockSpec((tm, d), lambda i, *_: (i, 0)),
        scratch_shapes=[
            pltpu.SMEM((2, TOP_K, tm), jnp.int32),
            pltpu.SMEM((2, TOP_K, tm), F32),
            pltpu.VMEM((2, LOCAL_GROUPS, GROUP_SUBLANES, LANES), F32),
            pltpu.VMEM((TILE_GROUPS, GROUP_SUBLANES, LANES), F32),
            pltpu.SemaphoreType.DMA((2,)),
            pltpu.SemaphoreType.DMA((2,)),
            pltpu.SemaphoreType.DMA((2,)),
        ],
    )
    return pl.pallas_call(
        _combine_body,
        grid_spec=grid_spec,
        out_shape=jax.ShapeDtypeStruct(x2.shape, F32),
        compiler_params=pltpu.CompilerParams(
            dimension_semantics=("arbitrary",), vmem_limit_bytes=VMEM_LIMIT),
        name="moe_combine_norm",
    )(tabs["nseg"], tabs["local_start"], tabs["global_start"], q, wts, y, x2, g.reshape(1, -1))


def _moe_and_final_norm(x2, ffn_g, w_router, w1, w3, w2, final_g):
    t, d = x2.shape
    nt = t // MOE_TILE
    max_rows = TOP_K * t + nt * N_EXPERTS * (SUBLANES - 1) + N_EXPERTS * (MOE_BLOCK - 1)
    n_blocks = -(-max_rows // MOE_BLOCK)
    q, wts, cnt3 = _router(x2, ffn_g, w_router)
    tabs = _segment_tables(cnt3[:, :, 0])
    block_start = jnp.arange(n_blocks, dtype=jnp.int32) * BLOCK_GROUPS
    block_e = jnp.minimum(jnp.sum(block_start[:, None] >= tabs["pend"][None, :], axis=1),
                          N_EXPERTS - 1).astype(jnp.int32)
    xs = _dispatch(tabs, q, x2, ffn_g, n_blocks * BLOCK_GROUPS)
    y = _expert_ffn(block_e, tabs["nused"], xs, w1, w3, w2)
    return _combine(tabs, q, wts, y, x2, final_g)


def kernel(x, mix_norm_g, w_in, conv_w, conv_b, conv_ln_g, conv_ln_b, conv_w_out, pool_w, pool_scale,
           sgu_ln_g, sgu_ln_b, sgu_w_s, sgu_b_s, sgu_w_out, w_out, ffn_norm_g, dense_w1, dense_w3,
           dense_w2, router_w, expert_w1, expert_w3, expert_w2, final_norm_g):
    b, s, d = x.shape

    def mixer(xx, l):
        return _mixer_layer(xx, mix_norm_g[l], w_in[l], conv_w[l], conv_b[l], conv_ln_g[l], conv_ln_b[l],
                            conv_w_out[l], pool_w[l], pool_scale[l], sgu_ln_g[l], sgu_ln_b[l],
                            sgu_w_s[l], sgu_b_s[l], sgu_w_out[l], w_out[l])

    x = mixer(x, 0)
    x2 = _dense_ffn(x.reshape(b * s, d), ffn_norm_g[0], dense_w1[0], dense_w3[0], dense_w2[0])
    x = mixer(x2.reshape(b, s, d), 1)
    out = _moe_and_final_norm(x.reshape(b * s, d), ffn_norm_g[1], router_w[0], expert_w1[0],
                              expert_w3[0], expert_w2[0], final_norm_g)
    return out.reshape(b, s, d)
```

```python
import jax
import jax.numpy as jnp
from jax import lax
from jax.experimental import pallas as pl
from jax.experimental.pallas import tpu as pltpu

F32 = jnp.float32
BF16 = jnp.bfloat16

D_MODEL = 1024
DEPTH = 2
D_CONV = 512
CONV_WIDTH = 31
D_POOL = 512
POOL_WINDOWS = (2, 4, 8, 16)
N_POOL_GROUPS = 4
POOL_GROUP = D_POOL // N_POOL_GROUPS
POOL_OUT_GROUP = D_MODEL // N_POOL_GROUPS
D_SGU = 512
SGU_HEADS = 4
SGU_HEAD_DIM = D_SGU // SGU_HEADS
CHUNK = 128
SPLIT_A = 2 * D_CONV
SPLIT_B = SPLIT_A + D_POOL
SPLIT_U = SPLIT_B + D_SGU
SPLIT_V = SPLIT_U + D_SGU
GATE_A = SPLIT_V
GATE_B = GATE_A + D_MODEL
GATE_C = GATE_B + D_MODEL
D_IN = GATE_C + D_MODEL
D_FF = 2816
N_EXPERTS = 8
TOP_K = 2
D_EXPERT = 3584
MOE_BLOCK = 512
EPS = 1e-6

SUBLANES = 8
LANES = 128
LANE_TILES = D_MODEL // LANES
HALO = 32
SEQ_TILE = 512
CONV_ROWS = 32
FFN_TILE = 512
MOE_TILE = 512
EXPERT_CHUNK = 512
VMEM_LIMIT = 56 * 1024 * 1024

GROUP_SUBLANES = LANE_TILES * SUBLANES
TILE_GROUPS = MOE_TILE // SUBLANES
BLOCK_GROUPS = MOE_BLOCK // SUBLANES
LOCAL_GROUPS = TOP_K * TILE_GROUPS + N_EXPERTS
SEG_BITS = TILE_GROUPS.bit_length()
TILE_CHOICES = TOP_K * MOE_TILE
BLOCK_ROWS = BLOCK_GROUPS * GROUP_SUBLANES


def _rms(x, g):
    ms = jnp.mean(x * x, axis=-1, keepdims=True)
    return x * lax.rsqrt(ms + EPS) * g


def _layernorm(x, g, b):
    mu = jnp.mean(x, axis=-1, keepdims=True)
    xc = x - mu
    var = jnp.mean(xc * xc, axis=-1, keepdims=True)
    return xc * lax.rsqrt(var + EPS) * g + b


def _sigmoid(x):
    return 1.0 / (1.0 + jnp.exp(-x))


def _const_spec(shape):
    return pl.BlockSpec(shape, lambda *_: (0,) * len(shape), pipeline_mode=pl.Buffered(1))


def _to_groups(v):
    rows = v.shape[0]
    return [v[:, j * LANES:(j + 1) * LANES].reshape(rows // SUBLANES, SUBLANES, LANES)
            for j in range(LANE_TILES)]


def _store_groups(ref, v):
    for j, piece in enumerate(_to_groups(v)):
        ref[:, j * SUBLANES:(j + 1) * SUBLANES, :] = piece


def _load_groups(ref):
    rows = ref.shape[0] * SUBLANES
    return jnp.concatenate(
        [ref[:, j * SUBLANES:(j + 1) * SUBLANES, :].reshape(rows, LANES) for j in range(LANE_TILES)],
        axis=-1)


def _row_at(offset):
    return pl.ds(offset, LANE_TILES, stride=SUBLANES)


def _mixer_body(x_ref, ng_ref, win_ref, cw_ref, cb_ref, clg_ref, clb_ref, cwo_ref,
                pw_ref, ps_ref, slg_ref, slb_ref, sws_ref, sbs_ref, swo_ref, wo_ref,
                o_ref, abuf, ash, cbuf, pbuf, tbuf, usbuf):
    ts = SEQ_TILE
    s = pl.program_id(1)

    @pl.when(s == 0)
    def _():
        abuf[0:HALO, :] = jnp.zeros((HALO, D_CONV), F32)
        pbuf[0:HALO, :] = jnp.zeros((HALO, D_POOL), F32)

    x = x_ref[0]
    h = _rms(x, ng_ref[...]).astype(BF16)

    def proj(lo, hi):
        return jnp.dot(h, win_ref[:, lo:hi], preferred_element_type=F32)

    ain = proj(0, SPLIT_A)
    abuf[HALO:HALO + ts, :] = ain[:, :D_CONV] * _sigmoid(ain[:, D_CONV:])
    for r in range(1, SUBLANES):
        ash[r - 1] = abuf[r:r + ts + HALO - SUBLANES, :]

    for r0 in range(0, ts, CONV_ROWS):
        acc = jnp.broadcast_to(cb_ref[...], (CONV_ROWS, D_CONV))
        for k in range(CONV_WIDTH):
            q, r = divmod(HALO - (CONV_WIDTH - 1) + k, SUBLANES)
            start = r0 + SUBLANES * q
            if r == 0:
                win = abuf[start:start + CONV_ROWS, :]
            else:
                win = ash[r - 1, start:start + CONV_ROWS, :]
            acc = acc + win * cw_ref[k:k + 1, :]
        cbuf[r0:r0 + CONV_ROWS, :] = acc
    abuf[0:HALO, :] = abuf[ts:ts + HALO, :]

    ca = _layernorm(cbuf[...], clg_ref[...], clb_ref[...])
    ca = (ca * _sigmoid(ca)).astype(BF16)
    y_a = jnp.dot(ca, cwo_ref[...], preferred_element_type=F32)
    mixed = _sigmoid(proj(GATE_A, GATE_B)) * y_a

    p = proj(SPLIT_A, SPLIT_B)
    pbuf[HALO:HALO + ts, :] = p
    tbuf[0:SUBLANES, :] = jnp.zeros((SUBLANES, POOL_GROUP), F32)
    pos1 = (s * ts + 1 + lax.broadcasted_iota(jnp.int32, (ts, 1), 0))
    ext = ts + HALO
    yb_parts = []
    for g, w in enumerate(POOL_WINDOWS):
        lo = g * POOL_GROUP
        cur = pbuf[:, lo:lo + POOL_GROUP]
        sh = 1
        while sh < w:
            tbuf[SUBLANES:SUBLANES + ext, :] = cur
            cur = cur + tbuf[SUBLANES - sh:SUBLANES - sh + ext, :]
            sh *= 2
        cnt = jnp.minimum(pos1, w).astype(F32)
        pooled = cur[HALO:, :] / cnt - p[:, lo:lo + POOL_GROUP]
        yb_parts.append(jnp.dot(pooled.astype(BF16), pw_ref[g], preferred_element_type=F32))
    pbuf[0:HALO, :] = pbuf[ts:ts + HALO, :]
    y_b = jnp.concatenate(yb_parts, axis=-1) * ps_ref[...]
    mixed = mixed + _sigmoid(proj(GATE_B, GATE_C)) * y_b

    u = proj(SPLIT_B, SPLIT_U)
    vn = _layernorm(proj(SPLIT_U, SPLIT_V), slg_ref[...], slb_ref[...]).astype(BF16)
    row = lax.broadcasted_iota(jnp.int32, (CHUNK, CHUNK), 0)
    col = lax.broadcasted_iota(jnp.int32, (CHUNK, CHUNK), 1)
    for hh in range(SGU_HEADS):
        wm = jnp.where(col <= row, sws_ref[hh], 0.0).astype(BF16)
        c0 = hh * SGU_HEAD_DIM
        for c in range(ts // CHUNK):
            r0 = c * CHUNK
            sp = jnp.dot(wm, vn[r0:r0 + CHUNK, c0:c0 + SGU_HEAD_DIM], preferred_element_type=F32)
            sp = sp + sbs_ref[hh]
            usbuf[r0:r0 + CHUNK, c0:c0 + SGU_HEAD_DIM] = (
                u[r0:r0 + CHUNK, c0:c0 + SGU_HEAD_DIM] * sp).astype(BF16)
    y_c = jnp.dot(usbuf[...], swo_ref[...], preferred_element_type=F32)
    mixed = mixed + _sigmoid(proj(GATE_C, D_IN)) * y_c

    o_ref[0] = x + jnp.dot(mixed.astype(BF16), wo_ref[...], preferred_element_type=F32)


def _mixer_layer(x, ng, w_in, cw, cb, clg, clb, cwo, pw, ps, slg, slb, sws, sbs, swo, wo):
    b, s, d = x.shape
    ts = SEQ_TILE
    row = lambda v: v.reshape(1, -1)
    sbs_b = jnp.broadcast_to(sbs[:, :, None], (SGU_HEADS, CHUNK, SGU_HEAD_DIM))
    args = (x, row(ng), w_in.astype(BF16), cw, row(cb), row(clg), row(clb), cwo.astype(BF16),
            pw.astype(BF16), row(ps), row(slg), row(slb), sws, sbs_b, swo.astype(BF16), wo.astype(BF16))
    x_spec = pl.BlockSpec((1, ts, d), lambda bi, si: (bi, si, 0))
    in_specs = [x_spec] + [_const_spec(a.shape) for a in args[1:]]
    return pl.pallas_call(
        _mixer_body,
        grid=(b, s // ts),
        in_specs=in_specs,
        out_specs=x_spec,
        out_shape=jax.ShapeDtypeStruct(x.shape, F32),
        scratch_shapes=[
            pltpu.VMEM((ts + HALO, D_CONV), F32),
            pltpu.VMEM((SUBLANES - 1, ts + HALO - SUBLANES, D_CONV), F32),
            pltpu.VMEM((ts, D_CONV), F32),
            pltpu.VMEM((ts + HALO, D_POOL), F32),
            pltpu.VMEM((SUBLANES + ts + HALO, POOL_GROUP), F32),
            pltpu.VMEM((ts, D_SGU), BF16),
        ],
        compiler_params=pltpu.CompilerParams(
            dimension_semantics=("arbitrary", "arbitrary"), vmem_limit_bytes=VMEM_LIMIT),
        name="token_mixer",
    )(*args)


def _dense_ffn_body(x_ref, g_ref, w1_ref, w3_ref, w2_ref, o_ref):
    x = x_ref[...]
    h = _rms(x, g_ref[...]).astype(BF16)
    a = jnp.dot(h, w1_ref[...], preferred_element_type=F32)
    b = jnp.dot(h, w3_ref[...], preferred_element_type=F32)
    act = (a * _sigmoid(a) * b).astype(BF16)
    o_ref[...] = x + jnp.dot(act, w2_ref[...], preferred_element_type=F32)


def _dense_ffn(x2, g, w1, w3, w2):
    t, d = x2.shape
    tm = FFN_TILE
    args = (x2, g.reshape(1, -1), w1.astype(BF16), w3.astype(BF16), w2.astype(BF16))
    x_spec = pl.BlockSpec((tm, d), lambda i: (i, 0))
    return pl.pallas_call(
        _dense_ffn_body,
        grid=(t // tm,),
        in_specs=[x_spec] + [_const_spec(a.shape) for a in args[1:]],
        out_specs=x_spec,
        out_shape=jax.ShapeDtypeStruct(x2.shape, F32),
        compiler_params=pltpu.CompilerParams(
            dimension_semantics=("arbitrary",), vmem_limit_bytes=VMEM_LIMIT),
        name="dense_ffn",
    )(*args)


def _router_body(x_ref, g_ref, wr_ref, tri_ref, q_ref, wt_ref, cnt_ref):
    tm = MOE_TILE
    h = _rms(x_ref[...], g_ref[...])
    logits = lax.dot_general(wr_ref[...], h, (((1,), (1,)), ((), ())),
                             precision=lax.Precision.HIGHEST, preferred_element_type=F32)
    eid = lax.broadcasted_iota(jnp.int32, logits.shape, 0)
    m1 = jnp.max(logits, axis=0, keepdims=True)
    i1 = jnp.min(jnp.where(logits == m1, eid, N_EXPERTS), axis=0, keepdims=True)
    rest = jnp.where(eid == i1, -jnp.inf, logits)
    m2 = jnp.max(rest, axis=0, keepdims=True)
    i2 = jnp.min(jnp.where(rest == m2, eid, N_EXPERTS), axis=0, keepdims=True)
    e2 = jnp.exp(m2 - m1)
    den = 1.0 + e2
    wt_ref[0] = jnp.concatenate([1.0 / den, e2 / den], axis=1)

    oh1 = eid == i1
    oh2 = eid == i2
    onehot = jnp.where(oh1 | oh2, 1.0, 0.0)
    incl = jnp.dot(onehot.astype(BF16), tri_ref[...], preferred_element_type=F32)
    cnt = jnp.sum(onehot, axis=1, keepdims=True).astype(jnp.int32)
    seg = (lax.shift_right_logical(cnt + (SUBLANES - 1), 3) * SUBLANES).astype(F32)
    erow = lax.broadcasted_iota(jnp.int32, (N_EXPERTS, 1), 0)
    seg_start = jnp.zeros((N_EXPERTS, 1), F32)
    for e in range(N_EXPERTS - 1):
        seg_start = seg_start + jnp.where(erow > e, seg[e:e + 1, :], 0.0)
    base = seg_start + (incl - onehot)
    q1 = jnp.sum(jnp.where(oh1, base, 0.0), axis=0, keepdims=True)
    q2 = jnp.sum(jnp.where(oh2, base, 0.0), axis=0, keepdims=True)
    q = jnp.concatenate([q1, q2], axis=1).astype(jnp.int32)
    q_ref[0] = (lax.shift_right_logical(q, 3) * GROUP_SUBLANES + lax.bitwise_and(q, SUBLANES - 1))
    cnt_ref[0] = jnp.broadcast_to(cnt, (N_EXPERTS, LANES))


def _router(x2, g, w_router):
    t, d = x2.shape
    tm = MOE_TILE
    nt = t // tm
    tri = jnp.triu(jnp.ones((tm, tm), BF16))
    tile_spec = pl.BlockSpec((1, 1, TOP_K * tm), lambda i: (i, 0, 0))
    return pl.pallas_call(
        _router_body,
        grid=(nt,),
        in_specs=[pl.BlockSpec((tm, d), lambda i: (i, 0)),
                  _const_spec((1, d)), _const_spec((N_EXPERTS, d)), _const_spec((tm, tm))],
        out_specs=[tile_spec, tile_spec, pl.BlockSpec((1, N_EXPERTS, LANES), lambda i: (i, 0, 0))],
        out_shape=[jax.ShapeDtypeStruct((nt, 1, TOP_K * tm), jnp.int32),
                   jax.ShapeDtypeStruct((nt, 1, TOP_K * tm), F32),
                   jax.ShapeDtypeStruct((nt, N_EXPERTS, LANES), jnp.int32)],
        compiler_params=pltpu.CompilerParams(dimension_semantics=("arbitrary",)),
        name="moe_router",
    )(x2, g.reshape(1, -1), w_router.T, tri)


def _segment_tables(cnt):
    seg = (cnt + (SUBLANES - 1)) // SUBLANES
    local_start = jnp.cumsum(seg, axis=1) - seg
    total = jnp.sum(seg, axis=0)
    padded = (total + BLOCK_GROUPS - 1) // BLOCK_GROUPS * BLOCK_GROUPS
    pend = jnp.cumsum(padded)
    pstart = pend - padded
    global_start = pstart[None, :] + jnp.cumsum(seg, axis=0) - seg
    nused = (pend[-1] // BLOCK_GROUPS).astype(jnp.int32).reshape(1)
    return dict(
        nseg=seg.reshape(-1).astype(jnp.int32),
        local_start=local_start.reshape(-1).astype(jnp.int32),
        global_start=global_start.reshape(-1).astype(jnp.int32),
        tail_start=(pstart + total).astype(jnp.int32),
        tail_len=(padded - total).astype(jnp.int32),
        pend=pend.astype(jnp.int32),
        nused=nused)


def _segment_pieces(nseg, local_start, global_start, tile, make_copy, act):
    for e in range(N_EXPERTS):
        k = tile * N_EXPERTS + e
        ng = nseg[k]
        l0 = local_start[k]
        g0 = global_start[k]
        for b in reversed(range(SEG_BITS)):
            size = 1 << b
            off = lax.shift_left(lax.shift_right_logical(ng, b + 1), b + 1)

            @pl.when(lax.bitwise_and(lax.shift_right_logical(ng, b), 1) == 1)
            def _():
                act(make_copy(l0 + off, g0 + off, size))


def _dispatch_body(ns_ref, ls_ref, gs_ref, ts_ref, tl_ref, nu_ref, q_hbm, x_ref, g_ref, xs_hbm,
                   q_smem, z, lbuf, zbuf, qsem, dsem, zsem):
    i = pl.program_id(0)
    n = pl.num_programs(0)
    slot = lax.rem(i, 2)

    def q_copy(tile, sl):
        return pltpu.make_async_copy(q_hbm.at[tile, 0], q_smem.at[pl.ds(sl * TILE_CHOICES, TILE_CHOICES)],
                                     qsem.at[sl])

    def seg_copy(sl):
        def make(loc, glob, size):
            return pltpu.make_async_copy(
                lbuf.at[pl.ds((sl * LOCAL_GROUPS + loc) * GROUP_SUBLANES, size * GROUP_SUBLANES)],
                xs_hbm.at[pl.ds(glob * GROUP_SUBLANES, size * GROUP_SUBLANES)], dsem.at[sl])
        return make

    @pl.when(i == 0)
    def _():
        q_copy(0, 0).start()
        lbuf[...] = jnp.zeros(lbuf.shape, F32)

    q_copy(i, slot).wait()

    @pl.when(i + 1 < n)
    def _():
        q_copy(i + 1, 1 - slot).start()

    _store_groups(z, _rms(x_ref[...], g_ref[...]))

    qbase = slot * TILE_CHOICES
    lbase = slot * (LOCAL_GROUPS * GROUP_SUBLANES)

    def scatter(ii, carry):
        for s in range(SUBLANES):
            t = ii * SUBLANES + s
            row = z[ii, pl.ds(s, LANE_TILES, stride=SUBLANES), :]
            for k in range(TOP_K):
                lbuf[_row_at(q_smem[qbase + k * MOE_TILE + t] + lbase), :] = row
        return carry

    lax.fori_loop(0, TILE_GROUPS, scatter, 0)

    @pl.when(i > 0)
    def _():
        _segment_pieces(ns_ref, ls_ref, gs_ref, i - 1, seg_copy(1 - slot), lambda c: c.wait())

    _segment_pieces(ns_ref, ls_ref, gs_ref, i, seg_copy(slot), lambda c: c.start())

    @pl.when(i == n - 1)
    def _():
        _segment_pieces(ns_ref, ls_ref, gs_ref, i, seg_copy(slot), lambda c: c.wait())
        zbuf[...] = jnp.zeros(zbuf.shape, F32)

        def tail_pieces(act):
            for e in range(N_EXPERTS):
                ng = tl_ref[e]
                g0 = ts_ref[e]
                for b in reversed(range(SEG_BITS)):
                    size = 1 << b
                    off = lax.shift_left(lax.shift_right_logical(ng, b + 1), b + 1)

                    @pl.when(lax.bitwise_and(lax.shift_right_logical(ng, b), 1) == 1)
                    def _():
                        act(pltpu.make_async_copy(
                            zbuf.at[pl.ds(0, size * GROUP_SUBLANES)],
                            xs_hbm.at[pl.ds((g0 + off) * GROUP_SUBLANES, size * GROUP_SUBLANES)],
                            zsem.at[0]))

        tail_pieces(lambda c: c.start())
        tail_pieces(lambda c: c.wait())

        def zero_block(blk, carry):
            c = pltpu.make_async_copy(zbuf, xs_hbm.at[pl.ds(blk * BLOCK_ROWS, BLOCK_ROWS)], zsem.at[0])
            c.start()
            c.wait()
            return carry

        lax.fori_loop(nu_ref[0], xs_hbm.shape[0] // BLOCK_ROWS, zero_block, 0)


def _dispatch(tabs, q, x2, g, n_groups):
    t, d = x2.shape
    tm = MOE_TILE
    nt = t // tm
    grid_spec = pltpu.PrefetchScalarGridSpec(
        num_scalar_prefetch=6,
        grid=(nt,),
        in_specs=[pl.BlockSpec(memory_space=pl.ANY),
                  pl.BlockSpec((tm, d), lambda i, *_: (i, 0)),
                  pl.BlockSpec((1, d), lambda i, *_: (0, 0), pipeline_mode=pl.Buffered(1))],
        out_specs=pl.BlockSpec(memory_space=pl.ANY),
        scratch_shapes=[
            pltpu.SMEM((2 * TILE_CHOICES,), jnp.int32),
            pltpu.VMEM((TILE_GROUPS, GROUP_SUBLANES, LANES), F32),
            pltpu.VMEM((2 * LOCAL_GROUPS * GROUP_SUBLANES, LANES), F32),
            pltpu.VMEM((BLOCK_ROWS, LANES), F32),
            pltpu.SemaphoreType.DMA((2,)),
            pltpu.SemaphoreType.DMA((2,)),
            pltpu.SemaphoreType.DMA((1,)),
        ],
    )
    return pl.pallas_call(
        _dispatch_body,
        grid_spec=grid_spec,
        out_shape=jax.ShapeDtypeStruct((n_groups * GROUP_SUBLANES, LANES), F32),
        compiler_params=pltpu.CompilerParams(
            dimension_semantics=("arbitrary",), vmem_limit_bytes=VMEM_LIMIT),
        name="moe_dispatch",
    )(tabs["nseg"], tabs["local_start"], tabs["global_start"], tabs["tail_start"], tabs["tail_len"],
      tabs["nused"], q, x2, g.reshape(1, -1))


def _expert_body(be_ref, nu_ref, xs_ref, w1_ref, w3_ref, w2_ref, o_ref, actbuf):
    i = pl.program_id(0)
    nused = nu_ref[0]

    @pl.when(i < nused)
    def _():
        h = _load_groups(xs_ref).astype(BF16)
        for j in range(D_EXPERT // EXPERT_CHUNK):
            cs = slice(j * EXPERT_CHUNK, (j + 1) * EXPERT_CHUNK)
            a = jnp.dot(h, w1_ref[0, :, cs], preferred_element_type=F32)
            b = jnp.dot(h, w3_ref[0, :, cs], preferred_element_type=F32)
            actbuf[:, cs] = (a * _sigmoid(a) * b).astype(BF16)
        _store_groups(o_ref, jnp.dot(actbuf[...], w2_ref[0], preferred_element_type=F32))

    @pl.when(i >= nused)
    def _():
        o_ref[...] = jnp.zeros(o_ref.shape, F32)


def _expert_ffn(block_e, nused, xs, w1, w3, w2):
    n_groups = xs.shape[0]
    nb = n_groups // BLOCK_GROUPS
    d = D_MODEL
    w13_spec = pl.BlockSpec((1, d, D_EXPERT), lambda i, be, nu: (be[i], 0, 0),
                            pipeline_mode=pl.Buffered(1))
    w2_spec = pl.BlockSpec((1, D_EXPERT, d), lambda i, be, nu: (be[i], 0, 0),
                           pipeline_mode=pl.Buffered(1))
    blk = (BLOCK_GROUPS, GROUP_SUBLANES, LANES)
    grid_spec = pltpu.PrefetchScalarGridSpec(
        num_scalar_prefetch=2,
        grid=(nb,),
        in_specs=[pl.BlockSpec(blk, lambda i, be, nu: (jnp.clip(i, 0, jnp.maximum(nu[0] - 1, 0)), 0, 0)),
                  w13_spec, w13_spec, w2_spec],
        out_specs=pl.BlockSpec(blk, lambda i, be, nu: (i, 0, 0)),
        scratch_shapes=[pltpu.VMEM((MOE_BLOCK, D_EXPERT), BF16)],
    )
    return pl.pallas_call(
        _expert_body,
        grid_spec=grid_spec,
        out_shape=jax.ShapeDtypeStruct(xs.shape, F32),
        compiler_params=pltpu.CompilerParams(
            dimension_semantics=("arbitrary",), vmem_limit_bytes=VMEM_LIMIT),
        name="moe_experts",
    )(block_e, nused, xs, w1.astype(BF16), w3.astype(BF16), w2.astype(BF16))


def _combine_body(ns_ref, ls_ref, gs_ref, q_hbm, w_hbm, y_hbm, x_ref, g_ref, o_ref,
                  q_smem, w_smem, ybuf, zo, qsem, wsem, ysem):
    i = pl.program_id(0)
    n = pl.num_programs(0)
    slot = lax.rem(i, 2)

    def meta_copies(tile, sl):
        dst = pl.ds(sl * TILE_CHOICES, TILE_CHOICES)
        return (pltpu.make_async_copy(q_hbm.at[tile, 0], q_smem.at[dst], qsem.at[sl]),
                pltpu.make_async_copy(w_hbm.at[tile, 0], w_smem.at[dst], wsem.at[sl]))

    def seg_copy(sl):
        def make(loc, glob, size):
            return pltpu.make_async_copy(
                y_hbm.at[pl.ds(glob * GROUP_SUBLANES, size * GROUP_SUBLANES)],
                ybuf.at[pl.ds((sl * LOCAL_GROUPS + loc) * GROUP_SUBLANES, size * GROUP_SUBLANES)],
                ysem.at[sl])
        return make

    def fetch(tile, sl):
        for c in meta_copies(tile, sl):
            c.start()
        _segment_pieces(ns_ref, ls_ref, gs_ref, tile, seg_copy(sl), lambda c: c.start())

    @pl.when(i == 0)
    def _():
        fetch(0, 0)

    @pl.when(i + 1 < n)
    def _():
        fetch(i + 1, 1 - slot)

    for c in meta_copies(i, slot):
        c.wait()
    _segment_pieces(ns_ref, ls_ref, gs_ref, i, seg_copy(slot), lambda c: c.wait())

    qbase = slot * TILE_CHOICES
    ybase = slot * (LOCAL_GROUPS * GROUP_SUBLANES)

    def gather(ii, carry):
        for s in range(SUBLANES):
            t0 = qbase + ii * SUBLANES + s
            t1 = t0 + MOE_TILE
            zo[ii, pl.ds(s, LANE_TILES, stride=SUBLANES), :] = (
                w_smem[t0] * ybuf[_row_at(q_smem[t0] + ybase), :]
                + w_smem[t1] * ybuf[_row_at(q_smem[t1] + ybase), :])
        return carry

    lax.fori_loop(0, TILE_GROUPS, gather, 0)
    o_ref[...] = _rms(x_ref[...] + _load_groups(zo), g_ref[...])


def _combine(tabs, q, wts, y, x2, g):
    t, d = x2.shape
    tm = MOE_TILE
    grid_spec = pltpu.PrefetchScalarGridSpec(
        num_scalar_prefetch=3,
        grid=(t // tm,),
        in_specs=[pl.BlockSpec(memory_space=pl.ANY),
                  pl.BlockSpec(memory_space=pl.ANY),
                  pl.BlockSpec(memory_space=pl.ANY),
                  pl.BlockSpec((tm, d), lambda i, *_: (i, 0)),
                  pl.BlockSpec((1, d), lambda i, *_: (0, 0), pipeline_mode=pl.Buffered(1))],
        out_specs=pl.BlockSpec((tm, d), lambda i, *_: (i, 0)),
        scratch_shapes=[
            pltpu.SMEM((2 * TILE_CHOICES,), jnp.int32),
            pltpu.SMEM((2 * TILE_CHOICES,), F32),
            pltpu.VMEM((2 * LOCAL_GROUPS * GROUP_SUBLANES, LANES), F32),
            pltpu.VMEM((TILE_GROUPS, GROUP_SUBLANES, LANES), F32),
            pltpu.SemaphoreType.DMA((2,)),
            pltpu.SemaphoreType.DMA((2,)),
            pltpu.SemaphoreType.DMA((2,)),
        ],
    )
    return pl.pallas_call(
        _combine_body,
        grid_spec=grid_spec,
        out_shape=jax.ShapeDtypeStruct(x2.shape, F32),
        compiler_params=pltpu.CompilerParams(
            dimension_semantics=("arbitrary",), vmem_limit_bytes=VMEM_LIMIT),
        name="moe_combine_norm",
    )(tabs["nseg"], tabs["local_start"], tabs["global_start"], q, wts, y, x2, g.reshape(1, -1))


def _moe_and_final_norm(x2, ffn_g, w_router, w1, w3, w2, final_g):
    t, d = x2.shape
    nt = t // MOE_TILE
    max_rows = TOP_K * t + nt * N_EXPERTS * (SUBLANES - 1) + N_EXPERTS * (MOE_BLOCK - 1)
    n_blocks = -(-max_rows // MOE_BLOCK)
    q, wts, cnt3 = _router(x2, ffn_g, w_router)
    tabs = _segment_tables(cnt3[:, :, 0])
    block_start = jnp.arange(n_blocks, dtype=jnp.int32) * BLOCK_GROUPS
    block_e = jnp.minimum(jnp.sum(block_start[:, None] >= tabs["pend"][None, :], axis=1),
                          N_EXPERTS - 1).astype(jnp.int32)
    xs = _dispatch(tabs, q, x2, ffn_g, n_blocks * BLOCK_GROUPS)
    y = _expert_ffn(block_e, tabs["nused"], xs.reshape(-1, GROUP_SUBLANES, LANES), w1, w3, w2)
    return _combine(tabs, q, wts, y.reshape(-1, LANES), x2, final_g)


def kernel(x, mix_norm_g, w_in, conv_w, conv_b, conv_ln_g, conv_ln_b, conv_w_out, pool_w, pool_scale,
           sgu_ln_g, sgu_ln_b, sgu_w_s, sgu_b_s, sgu_w_out, w_out, ffn_norm_g, dense_w1, dense_w3,
           dense_w2, router_w, expert_w1, expert_w3, expert_w2, final_norm_g):
    b, s, d = x.shape

    def mixer(xx, l):
        return _mixer_layer(xx, mix_norm_g[l], w_in[l], conv_w[l], conv_b[l], conv_ln_g[l], conv_ln_b[l],
                            conv_w_out[l], pool_w[l], pool_scale[l], sgu_ln_g[l], sgu_ln_b[l],
                            sgu_w_s[l], sgu_b_s[l], sgu_w_out[l], w_out[l])

    x = mixer(x, 0)
    x2 = _dense_ffn(x.reshape(b * s, d), ffn_norm_g[0], dense_w1[0], dense_w3[0], dense_w2[0])
    x = mixer(x2.reshape(b, s, d), 1)
    out = _moe_and_final_norm(x.reshape(b * s, d), ffn_norm_g[1], router_w[0], expert_w1[0],
                              expert_w3[0], expert_w2[0], final_norm_g)
    return out.reshape(b, s, d)
```

```python
import jax
import jax.numpy as jnp
from jax import lax
from jax.experimental import pallas as pl
from jax.experimental.pallas import tpu as pltpu

F32 = jnp.float32
BF16 = jnp.bfloat16

D_MODEL = 1024
DEPTH = 2
D_CONV = 512
CONV_WIDTH = 31
D_POOL = 512
POOL_WINDOWS = (2, 4, 8, 16)
N_POOL_GROUPS = 4
POOL_GROUP = D_POOL // N_POOL_GROUPS
POOL_OUT_GROUP = D_MODEL // N_POOL_GROUPS
D_SGU = 512
SGU_HEADS = 4
SGU_HEAD_DIM = D_SGU // SGU_HEADS
CHUNK = 128
SPLIT_A = 2 * D_CONV
SPLIT_B = SPLIT_A + D_POOL
SPLIT_U = SPLIT_B + D_SGU
SPLIT_V = SPLIT_U + D_SGU
GATE_A = SPLIT_V
GATE_B = GATE_A + D_MODEL
GATE_C = GATE_B + D_MODEL
D_IN = GATE_C + D_MODEL
D_FF = 2816
N_EXPERTS = 8
TOP_K = 2
D_EXPERT = 3584
MOE_BLOCK = 512
EPS = 1e-6

SUBLANES = 8
LANES = 128
LANE_TILES = D_MODEL // LANES
HALO = 32
SEQ_TILE = 512
CONV_ROWS = 16
PROJ_CHUNK = 256
FFN_TILE = 512
MOE_TILE = 512
EXPERT_CHUNK = 512
VMEM_LIMIT = 56 * 1024 * 1024

GROUP_SUBLANES = LANE_TILES * SUBLANES
TILE_GROUPS = MOE_TILE // SUBLANES
BLOCK_GROUPS = MOE_BLOCK // SUBLANES
LOCAL_GROUPS = TOP_K * TILE_GROUPS + N_EXPERTS
SEG_BITS = TILE_GROUPS.bit_length()
TILE_CHOICES = TOP_K * MOE_TILE
BLOCK_ROWS = BLOCK_GROUPS * GROUP_SUBLANES


def _rms(x, g):
    ms = jnp.mean(x * x, axis=-1, keepdims=True)
    return x * lax.rsqrt(ms + EPS) * g


def _layernorm(x, g, b):
    mu = jnp.mean(x, axis=-1, keepdims=True)
    xc = x - mu
    var = jnp.mean(xc * xc, axis=-1, keepdims=True)
    return xc * lax.rsqrt(var + EPS) * g + b


def _sigmoid(x):
    return 1.0 / (1.0 + jnp.exp(-x))


def _const_spec(shape):
    return pl.BlockSpec(shape, lambda *_: (0,) * len(shape), pipeline_mode=pl.Buffered(1))


def _to_groups(v):
    rows = v.shape[0]
    return [v[:, j * LANES:(j + 1) * LANES].reshape(rows // SUBLANES, SUBLANES, LANES)
            for j in range(LANE_TILES)]


def _store_groups(ref, v):
    for j, piece in enumerate(_to_groups(v)):
        ref[:, j * SUBLANES:(j + 1) * SUBLANES, :] = piece


def _load_groups(ref):
    rows = ref.shape[0] * SUBLANES
    return jnp.concatenate(
        [ref[:, j * SUBLANES:(j + 1) * SUBLANES, :].reshape(rows, LANES) for j in range(LANE_TILES)],
        axis=-1)


def _row_at(offset):
    return pl.ds(offset, LANE_TILES, stride=SUBLANES)


def _mixer_body(x_ref, ng_ref, win_ref, cw_ref, cb_ref, clg_ref, clb_ref, cwo_ref,
                pw_ref, ps_ref, slg_ref, slb_ref, sws_ref, sbs_ref, swo_ref, wo_ref,
                o_ref, abuf, ash, cbuf, pbuf, tbuf, usbuf, uvbuf, gbuf):
    ts = SEQ_TILE
    s = pl.program_id(1)

    @pl.when(s == 0)
    def _():
        abuf[0:HALO, :] = jnp.zeros((HALO, D_CONV), F32)
        pbuf[0:HALO, :] = jnp.zeros((HALO, D_POOL), F32)

    x = x_ref[0]
    h = _rms(x, ng_ref[...]).astype(BF16)

    def proj(lo, hi):
        return jnp.dot(h, win_ref[:, lo:hi], preferred_element_type=F32)

    def project_puv(c):
        lo = SPLIT_A + c * PROJ_CHUNK
        if lo < SPLIT_B:
            pbuf[HALO:HALO + ts, lo - SPLIT_A:lo - SPLIT_A + PROJ_CHUNK] = proj(lo, lo + PROJ_CHUNK)
        else:
            uvbuf[:, lo - SPLIT_B:lo - SPLIT_B + PROJ_CHUNK] = proj(lo, lo + PROJ_CHUNK)

    def project_gate(c):
        lo = c * PROJ_CHUNK
        gbuf[:, lo:lo + PROJ_CHUNK] = _sigmoid(proj(GATE_A + lo, GATE_A + lo + PROJ_CHUNK))

    n_puv = (SPLIT_V - SPLIT_A) // PROJ_CHUNK
    n_gate = (D_IN - GATE_A) // PROJ_CHUNK

    ain = proj(0, SPLIT_A)
    abuf[HALO:HALO + ts, :] = ain[:, :D_CONV] * _sigmoid(ain[:, D_CONV:])
    for r in range(1, SUBLANES):
        ash[r - 1] = abuf[r:r + ts + HALO - SUBLANES, :]
        if r - 1 < n_puv:
            project_puv(r - 1)
    for c in range(SUBLANES - 1, n_puv):
        project_puv(c)

    n_conv = ts // CONV_ROWS
    groups = CONV_ROWS // SUBLANES
    for ci in range(n_conv):
        r0 = ci * CONV_ROWS
        acc = jnp.broadcast_to(cb_ref[...][None], (groups, SUBLANES, D_CONV))
        for k in range(CONV_WIDTH):
            q, r = divmod(HALO - (CONV_WIDTH - 1) + k, SUBLANES)
            start = r0 + SUBLANES * q
            if r == 0:
                win = abuf[start:start + CONV_ROWS, :]
            else:
                win = ash[r - 1, start:start + CONV_ROWS, :]
            acc = acc + win.reshape(groups, SUBLANES, D_CONV) * cw_ref[k][None]
        cbuf[r0:r0 + CONV_ROWS, :] = acc.reshape(CONV_ROWS, D_CONV)
        for c in range(n_gate):
            if ((c + 1) * n_conv) // n_gate - 1 == ci:
                project_gate(c)
    abuf[0:HALO, :] = abuf[ts:ts + HALO, :]

    ca = _layernorm(cbuf[...], clg_ref[...], clb_ref[...])
    ca = (ca * _sigmoid(ca)).astype(BF16)
    y_a = jnp.dot(ca, cwo_ref[...], preferred_element_type=F32)
    mixed = gbuf[:, 0:D_MODEL] * y_a

    tbuf[0:SUBLANES, :] = jnp.zeros((SUBLANES, POOL_GROUP), F32)
    pos1 = (s * ts + 1 + lax.broadcasted_iota(jnp.int32, (ts, 1), 0))
    ext = ts + HALO
    yb_parts = []
    for g, w in enumerate(POOL_WINDOWS):
        lo = g * POOL_GROUP
        cur = pbuf[:, lo:lo + POOL_GROUP]
        sh = 1
        while sh < w:
            tbuf[SUBLANES:SUBLANES + ext, :] = cur
            cur = cur + tbuf[SUBLANES - sh:SUBLANES - sh + ext, :]
            sh *= 2
        cnt = jnp.minimum(pos1, w).astype(F32)
        pooled = cur[HALO:, :] / cnt - pbuf[HALO:HALO + ts, lo:lo + POOL_GROUP]
        yb_parts.append(jnp.dot(pooled.astype(BF16), pw_ref[g], preferred_element_type=F32))
    pbuf[0:HALO, :] = pbuf[ts:ts + HALO, :]
    y_b = jnp.concatenate(yb_parts, axis=-1) * ps_ref[...]
    mixed = mixed + gbuf[:, D_MODEL:2 * D_MODEL] * y_b

    vn = _layernorm(uvbuf[:, D_SGU:2 * D_SGU], slg_ref[...], slb_ref[...]).astype(BF16)
    row = lax.broadcasted_iota(jnp.int32, (CHUNK, CHUNK), 0)
    col = lax.broadcasted_iota(jnp.int32, (CHUNK, CHUNK), 1)
    for hh in range(SGU_HEADS):
        wm = jnp.where(col <= row, sws_ref[hh], 0.0).astype(BF16)
        c0 = hh * SGU_HEAD_DIM
        for c in range(ts // CHUNK):
            r0 = c * CHUNK
            sp = jnp.dot(wm, vn[r0:r0 + CHUNK, c0:c0 + SGU_HEAD_DIM], preferred_element_type=F32)
            sp = sp + sbs_ref[hh]
            usbuf[r0:r0 + CHUNK, c0:c0 + SGU_HEAD_DIM] = (
                uvbuf[r0:r0 + CHUNK, c0:c0 + SGU_HEAD_DIM] * sp).astype(BF16)
    y_c = jnp.dot(usbuf[...], swo_ref[...], preferred_element_type=F32)
    mixed = mixed + gbuf[:, 2 * D_MODEL:3 * D_MODEL] * y_c

    o_ref[0] = x + jnp.dot(mixed.astype(BF16), wo_ref[...], preferred_element_type=F32)


def _mixer_layer(x, ng, w_in, cw, cb, clg, clb, cwo, pw, ps, slg, slb, sws, sbs, swo, wo):
    b, s, d = x.shape
    ts = SEQ_TILE
    row = lambda v: v.reshape(1, -1)
    sbs_b = jnp.broadcast_to(sbs[:, :, None], (SGU_HEADS, CHUNK, SGU_HEAD_DIM))
    cw_b = jnp.broadcast_to(cw[:, None, :], (CONV_WIDTH, SUBLANES, D_CONV))
    cb_b = jnp.broadcast_to(cb[None, :], (SUBLANES, D_CONV))
    args = (x, row(ng), w_in.astype(BF16), cw_b, cb_b, row(clg), row(clb), cwo.astype(BF16),
            pw.astype(BF16), row(ps), row(slg), row(slb), sws, sbs_b, swo.astype(BF16), wo.astype(BF16))
    x_spec = pl.BlockSpec((1, ts, d), lambda bi, si: (bi, si, 0))
    in_specs = [x_spec] + [_const_spec(a.shape) for a in args[1:]]
    return pl.pallas_call(
        _mixer_body,
        grid=(b, s // ts),
        in_specs=in_specs,
        out_specs=x_spec,
        out_shape=jax.ShapeDtypeStruct(x.shape, F32),
        scratch_shapes=[
            pltpu.VMEM((ts + HALO, D_CONV), F32),
            pltpu.VMEM((SUBLANES - 1, ts + HALO - SUBLANES, D_CONV), F32),
            pltpu.VMEM((ts, D_CONV), F32),
            pltpu.VMEM((ts + HALO, D_POOL), F32),
            pltpu.VMEM((SUBLANES + ts + HALO, POOL_GROUP), F32),
            pltpu.VMEM((ts, D_SGU), BF16),
            pltpu.VMEM((ts, 2 * D_SGU), F32),
            pltpu.VMEM((ts, 3 * D_MODEL), F32),
        ],
        compiler_params=pltpu.CompilerParams(
            dimension_semantics=("arbitrary", "arbitrary"), vmem_limit_bytes=VMEM_LIMIT),
        name="token_mixer",
    )(*args)


def _dense_ffn_body(x_ref, g_ref, w1_ref, w3_ref, w2_ref, o_ref):
    x = x_ref[...]
    h = _rms(x, g_ref[...]).astype(BF16)
    a = jnp.dot(h, w1_ref[...], preferred_element_type=F32)
    b = jnp.dot(h, w3_ref[...], preferred_element_type=F32)
    act = (a * _sigmoid(a) * b).astype(BF16)
    o_ref[...] = x + jnp.dot(act, w2_ref[...], preferred_element_type=F32)


def _dense_ffn(x2, g, w1, w3, w2):
    t, d = x2.shape
    tm = FFN_TILE
    args = (x2, g.reshape(1, -1), w1.astype(BF16), w3.astype(BF16), w2.astype(BF16))
    x_spec = pl.BlockSpec((tm, d), lambda i: (i, 0))
    return pl.pallas_call(
        _dense_ffn_body,
        grid=(t // tm,),
        in_specs=[x_spec] + [_const_spec(a.shape) for a in args[1:]],
        out_specs=x_spec,
        out_shape=jax.ShapeDtypeStruct(x2.shape, F32),
        compiler_params=pltpu.CompilerParams(
            dimension_semantics=("arbitrary",), vmem_limit_bytes=VMEM_LIMIT),
        name="dense_ffn",
    )(*args)


def _router_body(x_ref, g_ref, wr_ref, tri_ref, q_ref, wt_ref, cnt_ref):
    tm = MOE_TILE
    h = _rms(x_ref[...], g_ref[...])
    w = wr_ref[...]
    h_hi = h.astype(BF16)
    h_lo = (h - h_hi.astype(F32)).astype(BF16)
    w_hi = w.astype(BF16)
    w_lo = (w - w_hi.astype(F32)).astype(BF16)

    def dot_nt(a, b):
        return lax.dot_general(a, b, (((1,), (1,)), ((), ())), preferred_element_type=F32)

    logits = dot_nt(w_hi, h_hi) + (dot_nt(w_hi, h_lo) + dot_nt(w_lo, h_hi))
    eid = lax.broadcasted_iota(jnp.int32, logits.shape, 0)
    m1 = jnp.max(logits, axis=0, keepdims=True)
    i1 = jnp.min(jnp.where(logits == m1, eid, N_EXPERTS), axis=0, keepdims=True)
    rest = jnp.where(eid == i1, -jnp.inf, logits)
    m2 = jnp.max(rest, axis=0, keepdims=True)
    i2 = jnp.min(jnp.where(rest == m2, eid, N_EXPERTS), axis=0, keepdims=True)
    e2 = jnp.exp(m2 - m1)
    den = 1.0 + e2
    wt_ref[0] = jnp.concatenate([1.0 / den, e2 / den], axis=1)

    oh1 = eid == i1
    oh2 = eid == i2
    onehot = jnp.where(oh1 | oh2, 1.0, 0.0)
    incl = jnp.dot(onehot.astype(BF16), tri_ref[...], preferred_element_type=F32)
    cnt = jnp.sum(onehot, axis=1, keepdims=True).astype(jnp.int32)
    seg = (lax.shift_right_logical(cnt + (SUBLANES - 1), 3) * SUBLANES).astype(F32)
    erow = lax.broadcasted_iota(jnp.int32, (N_EXPERTS, 1), 0)
    seg_start = jnp.zeros((N_EXPERTS, 1), F32)
    for e in range(N_EXPERTS - 1):
        seg_start = seg_start + jnp.where(erow > e, seg[e:e + 1, :], 0.0)
    base = seg_start + (incl - onehot)
    q1 = jnp.sum(jnp.where(oh1, base, 0.0), axis=0, keepdims=True)
    q2 = jnp.sum(jnp.where(oh2, base, 0.0), axis=0, keepdims=True)
    q = jnp.concatenate([q1, q2], axis=1).astype(jnp.int32)
    q_ref[0] = (lax.shift_right_logical(q, 3) * GROUP_SUBLANES + lax.bitwise_and(q, SUBLANES - 1))
    cnt_ref[0] = jnp.broadcast_to(cnt, (N_EXPERTS, LANES))


def _router(x2, g, w_router):
    t, d = x2.shape
    tm = MOE_TILE
    nt = t // tm
    tri = jnp.triu(jnp.ones((tm, tm), BF16))
    tile_spec = pl.BlockSpec((1, 1, TOP_K * tm), lambda i: (i, 0, 0))
    return pl.pallas_call(
        _router_body,
        grid=(nt,),
        in_specs=[pl.BlockSpec((tm, d), lambda i: (i, 0)),
                  _const_spec((1, d)), _const_spec((N_EXPERTS, d)), _const_spec((tm, tm))],
        out_specs=[tile_spec, tile_spec, pl.BlockSpec((1, N_EXPERTS, LANES), lambda i: (i, 0, 0))],
        out_shape=[jax.ShapeDtypeStruct((nt, 1, TOP_K * tm), jnp.int32),
                   jax.ShapeDtypeStruct((nt, 1, TOP_K * tm), F32),
                   jax.ShapeDtypeStruct((nt, N_EXPERTS, LANES), jnp.int32)],
        compiler_params=pltpu.CompilerParams(dimension_semantics=("arbitrary",)),
        name="moe_router",
    )(x2, g.reshape(1, -1), w_router.T, tri)


def _segment_tables(cnt):
    seg = (cnt + (SUBLANES - 1)) // SUBLANES
    local_start = jnp.cumsum(seg, axis=1) - seg
    total = jnp.sum(seg, axis=0)
    padded = (total + BLOCK_GROUPS - 1) // BLOCK_GROUPS * BLOCK_GROUPS
    pend = jnp.cumsum(padded)
    pstart = pend - padded
    global_start = pstart[None, :] + jnp.cumsum(seg, axis=0) - seg
    nused = (pend[-1] // BLOCK_GROUPS).astype(jnp.int32).reshape(1)
    return dict(
        nseg=seg.reshape(-1).astype(jnp.int32),
        local_start=local_start.reshape(-1).astype(jnp.int32),
        global_start=global_start.reshape(-1).astype(jnp.int32),
        tail_start=(pstart + total).astype(jnp.int32),
        tail_len=(padded - total).astype(jnp.int32),
        pend=pend.astype(jnp.int32),
        nused=nused)


def _segment_pieces(nseg, local_start, global_start, tile, make_copy, act):
    for e in range(N_EXPERTS):
        k = tile * N_EXPERTS + e
        ng = nseg[k]
        l0 = local_start[k]
        g0 = global_start[k]
        for b in reversed(range(SEG_BITS)):
            size = 1 << b
            off = lax.shift_left(lax.shift_right_logical(ng, b + 1), b + 1)

            @pl.when(lax.bitwise_and(lax.shift_right_logical(ng, b), 1) == 1)
            def _():
                act(make_copy(l0 + off, g0 + off, size))


def _dispatch_body(ns_ref, ls_ref, gs_ref, ts_ref, tl_ref, nu_ref, q_hbm, x_ref, g_ref, xs_hbm,
                   q_smem, z, lbuf, zbuf, qsem, dsem, zsem):
    i = pl.program_id(0)
    n = pl.num_programs(0)
    slot = lax.rem(i, 2)

    def q_copy(tile, sl):
        return pltpu.make_async_copy(q_hbm.at[tile, 0], q_smem.at[pl.ds(sl * TILE_CHOICES, TILE_CHOICES)],
                                     qsem.at[sl])

    def seg_copy(sl):
        def make(loc, glob, size):
            return pltpu.make_async_copy(
                lbuf.at[pl.ds((sl * LOCAL_GROUPS + loc) * GROUP_SUBLANES, size * GROUP_SUBLANES)],
                xs_hbm.at[pl.ds(glob * GROUP_SUBLANES, size * GROUP_SUBLANES)], dsem.at[sl])
        return make

    @pl.when(i == 0)
    def _():
        q_copy(0, 0).start()
        lbuf[...] = jnp.zeros(lbuf.shape, F32)

    q_copy(i, slot).wait()

    @pl.when(i + 1 < n)
    def _():
        q_copy(i + 1, 1 - slot).start()

    _store_groups(z, _rms(x_ref[...], g_ref[...]))

    qbase = slot * TILE_CHOICES
    lbase = slot * (LOCAL_GROUPS * GROUP_SUBLANES)

    def scatter(ii, carry):
        for s in range(SUBLANES):
            t = ii * SUBLANES + s
            row = z[ii, pl.ds(s, LANE_TILES, stride=SUBLANES), :]
            for k in range(TOP_K):
                lbuf[_row_at(q_smem[qbase + k * MOE_TILE + t] + lbase), :] = row
        return carry

    lax.fori_loop(0, TILE_GROUPS, scatter, 0)

    @pl.when(i > 0)
    def _():
        _segment_pieces(ns_ref, ls_ref, gs_ref, i - 1, seg_copy(1 - slot), lambda c: c.wait())

    _segment_pieces(ns_ref, ls_ref, gs_ref, i, seg_copy(slot), lambda c: c.start())

    @pl.when(i == n - 1)
    def _():
        _segment_pieces(ns_ref, ls_ref, gs_ref, i, seg_copy(slot), lambda c: c.wait())
        zbuf[...] = jnp.zeros(zbuf.shape, F32)

        def tail_pieces(act):
            for e in range(N_EXPERTS):
                ng = tl_ref[e]
                g0 = ts_ref[e]
                for b in reversed(range(SEG_BITS)):
                    size = 1 << b
                    off = lax.shift_left(lax.shift_right_logical(ng, b + 1), b + 1)

                    @pl.when(lax.bitwise_and(lax.shift_right_logical(ng, b), 1) == 1)
                    def _():
                        act(pltpu.make_async_copy(
                            zbuf.at[pl.ds(0, size * GROUP_SUBLANES)],
                            xs_hbm.at[pl.ds((g0 + off) * GROUP_SUBLANES, size * GROUP_SUBLANES)],
                            zsem.at[0]))

        tail_pieces(lambda c: c.start())
        tail_pieces(lambda c: c.wait())

        def zero_block(blk, carry):
            c = pltpu.make_async_copy(zbuf, xs_hbm.at[pl.ds(blk * BLOCK_ROWS, BLOCK_ROWS)], zsem.at[0])
            c.start()
            c.wait()
            return carry

        lax.fori_loop(nu_ref[0], xs_hbm.shape[0] // BLOCK_ROWS, zero_block, 0)


def _dispatch(tabs, q, x2, g, n_groups):
    t, d = x2.shape
    tm = MOE_TILE
    nt = t // tm
    grid_spec = pltpu.PrefetchScalarGridSpec(
        num_scalar_prefetch=6,
        grid=(nt,),
        in_specs=[pl.BlockSpec(memory_space=pl.ANY),
                  pl.BlockSpec((tm, d), lambda i, *_: (i, 0)),
                  pl.BlockSpec((1, d), lambda i, *_: (0, 0), pipeline_mode=pl.Buffered(1))],
        out_specs=pl.BlockSpec(memory_space=pl.ANY),
        scratch_shapes=[
            pltpu.SMEM((2 * TILE_CHOICES,), jnp.int32),
            pltpu.VMEM((TILE_GROUPS, GROUP_SUBLANES, LANES), F32),
            pltpu.VMEM((2 * LOCAL_GROUPS * GROUP_SUBLANES, LANES), F32),
            pltpu.VMEM((BLOCK_ROWS, LANES), F32),
            pltpu.SemaphoreType.DMA((2,)),
            pltpu.SemaphoreType.DMA((2,)),
            pltpu.SemaphoreType.DMA((1,)),
        ],
    )
    return pl.pallas_call(
        _dispatch_body,
        grid_spec=grid_spec,
        out_shape=jax.ShapeDtypeStruct((n_groups * GROUP_SUBLANES, LANES), F32),
        compiler_params=pltpu.CompilerParams(
            dimension_semantics=("arbitrary",), vmem_limit_bytes=VMEM_LIMIT),
        name="moe_dispatch",
    )(tabs["nseg"], tabs["local_start"], tabs["global_start"], tabs["tail_start"], tabs["tail_len"],
      tabs["nused"], q, x2, g.reshape(1, -1))


def _expert_body(be_ref, nu_ref, xs_ref, w1_ref, w3_ref, w2_ref, o_ref, actbuf):
    i = pl.program_id(0)
    nused = nu_ref[0]

    @pl.when(i < nused)
    def _():
        h = _load_groups(xs_ref).astype(BF16)
        for j in range(D_EXPERT // EXPERT_CHUNK):
            cs = slice(j * EXPERT_CHUNK, (j + 1) * EXPERT_CHUNK)
            a = jnp.dot(h, w1_ref[0, :, cs], preferred_element_type=F32)
            b = jnp.dot(h, w3_ref[0, :, cs], preferred_element_type=F32)
            actbuf[:, cs] = (a * _sigmoid(a) * b).astype(BF16)
        _store_groups(o_ref, jnp.dot(actbuf[...], w2_ref[0], preferred_element_type=F32))

    @pl.when(i >= nused)
    def _():
        o_ref[...] = jnp.zeros(o_ref.shape, F32)


def _expert_ffn(block_e, nused, xs, w1, w3, w2):
    n_groups = xs.shape[0]
    nb = n_groups // BLOCK_GROUPS
    d = D_MODEL
    w13_spec = pl.BlockSpec((1, d, D_EXPERT), lambda i, be, nu: (be[i], 0, 0),
                            pipeline_mode=pl.Buffered(1))
    w2_spec = pl.BlockSpec((1, D_EXPERT, d), lambda i, be, nu: (be[i], 0, 0),
                           pipeline_mode=pl.Buffered(1))
    blk = (BLOCK_GROUPS, GROUP_SUBLANES, LANES)
    grid_spec = pltpu.PrefetchScalarGridSpec(
        num_scalar_prefetch=2,
        grid=(nb,),
        in_specs=[pl.BlockSpec(blk, lambda i, be, nu: (jnp.clip(i, 0, jnp.maximum(nu[0] - 1, 0)), 0, 0)),
                  w13_spec, w13_spec, w2_spec],
        out_specs=pl.BlockSpec(blk, lambda i, be, nu: (i, 0, 0)),
        scratch_shapes=[pltpu.VMEM((MOE_BLOCK, D_EXPERT), BF16)],
    )
    return pl.pallas_call(
        _expert_body,
        grid_spec=grid_spec,
        out_shape=jax.ShapeDtypeStruct(xs.shape, F32),
        compiler_params=pltpu.CompilerParams(
            dimension_semantics=("arbitrary",), vmem_limit_bytes=VMEM_LIMIT),
        name="moe_experts",
    )(block_e, nused, xs, w1.astype(BF16), w3.astype(BF16), w2.astype(BF16))


def _combine_body(ns_ref, ls_ref, gs_ref, q_hbm, w_hbm, y_hbm, x_ref, g_ref, o_ref,
                  q_smem, w_smem, ybuf, zo, qsem, wsem, ysem):
    i = pl.program_id(0)
    n = pl.num_programs(0)
    slot = lax.rem(i, 2)

    def meta_copies(tile, sl):
        dst = pl.ds(sl * TILE_CHOICES, TILE_CHOICES)
        return (pltpu.make_async_copy(q_hbm.at[tile, 0], q_smem.at[dst], qsem.at[sl]),
                pltpu.make_async_copy(w_hbm.at[tile, 0], w_smem.at[dst], wsem.at[sl]))

    def seg_copy(sl):
        def make(loc, glob, size):
            return pltpu.make_async_copy(
                y_hbm.at[pl.ds(glob * GROUP_SUBLANES, size * GROUP_SUBLANES)],
                ybuf.at[pl.ds((sl * LOCAL_GROUPS + loc) * GROUP_SUBLANES, size * GROUP_SUBLANES)],
                ysem.at[sl])
        return make

    def fetch(tile, sl):
        for c in meta_copies(tile, sl):
            c.start()
        _segment_pieces(ns_ref, ls_ref, gs_ref, tile, seg_copy(sl), lambda c: c.start())

    @pl.when(i == 0)
    def _():
        fetch(0, 0)

    @pl.when(i + 1 < n)
    def _():
        fetch(i + 1, 1 - slot)

    for c in meta_copies(i, slot):
        c.wait()
    _segment_pieces(ns_ref, ls_ref, gs_ref, i, seg_copy(slot), lambda c: c.wait())

    qbase = slot * TILE_CHOICES
    ybase = slot * (LOCAL_GROUPS * GROUP_SUBLANES)

    def gather(ii, carry):
        for s in range(SUBLANES):
            t0 = qbase + ii * SUBLANES + s
            t1 = t0 + MOE_TILE
            zo[ii, pl.ds(s, LANE_TILES, stride=SUBLANES), :] = (
                w_smem[t0] * ybuf[_row_at(q_smem[t0] + ybase), :]
                + w_smem[t1] * ybuf[_row_at(q_smem[t1] + ybase), :])
        return carry

    lax.fori_loop(0, TILE_GROUPS, gather, 0)
    o_ref[...] = _rms(x_ref[...] + _load_groups(zo), g_ref[...])


def _combine(tabs, q, wts, y, x2, g):
    t, d = x2.shape
    tm = MOE_TILE
    grid_spec = pltpu.PrefetchScalarGridSpec(
        num_scalar_prefetch=3,
        grid=(t // tm,),
        in_specs=[pl.BlockSpec(memory_space=pl.ANY),
                  pl.BlockSpec(memory_space=pl.ANY),
                  pl.BlockSpec(memory_space=pl.ANY),
                  pl.BlockSpec((tm, d), lambda i, *_: (i, 0)),
                  pl.BlockSpec((1, d), lambda i, *_: (0, 0), pipeline_mode=pl.Buffered(1))],
        out_specs=pl.BlockSpec((tm, d), lambda i, *_: (i, 0)),
        scratch_shapes=[
            pltpu.SMEM((2 * TILE_CHOICES,), jnp.int32),
            pltpu.SMEM((2 * TILE_CHOICES,), F32),
            pltpu.VMEM((2 * LOCAL_GROUPS * GROUP_SUBLANES, LANES), F32),
            pltpu.VMEM((TILE_GROUPS, GROUP_SUBLANES, LANES), F32),
            pltpu.SemaphoreType.DMA((2,)),
            pltpu.SemaphoreType.DMA((2,)),
            pltpu.SemaphoreType.DMA((2,)),
        ],
    )
    return pl.pallas_call(
        _combine_body,
        grid_spec=grid_spec,
        out_shape=jax.ShapeDtypeStruct(x2.shape, F32),
        compiler_params=pltpu.CompilerParams(
            dimension_semantics=("arbitrary",), vmem_limit_bytes=VMEM_LIMIT),
        name="moe_combine_norm",
    )(tabs["nseg"], tabs["local_start"], tabs["global_start"], q, wts, y, x2, g.reshape(1, -1))


def _moe_and_final_norm(x2, ffn_g, w_router, w1, w3, w2, final_g):
    t, d = x2.shape
    nt = t // MOE_TILE
    max_rows = TOP_K * t + nt * N_EXPERTS * (SUBLANES - 1) + N_EXPERTS * (MOE_BLOCK - 1)
    n_blocks = -(-max_rows // MOE_BLOCK)
    q, wts, cnt3 = _router(x2, ffn_g, w_router)
    tabs = _segment_tables(cnt3[:, :, 0])
    block_start = jnp.arange(n_blocks, dtype=jnp.int32) * BLOCK_GROUPS
    block_e = jnp.minimum(jnp.sum(block_start[:, None] >= tabs["pend"][None, :], axis=1),
                          N_EXPERTS - 1).astype(jnp.int32)
    xs = _dispatch(tabs, q, x2, ffn_g, n_blocks * BLOCK_GROUPS)
    y = _expert_ffn(block_e, tabs["nused"], xs.reshape(-1, GROUP_SUBLANES, LANES), w1, w3, w2)
    return _combine(tabs, q, wts, y.reshape(-1, LANES), x2, final_g)


def kernel(x, mix_norm_g, w_in, conv_w, conv_b, conv_ln_g, conv_ln_b, conv_w_out, pool_w, pool_scale,
           sgu_ln_g, sgu_ln_b, sgu_w_s, sgu_b_s, sgu_w_out, w_out, ffn_norm_g, dense_w1, dense_w3,
           dense_w2, router_w, expert_w1, expert_w3, expert_w2, final_norm_g):
    b, s, d = x.shape

    def mixer(xx, l):
        return _mixer_layer(xx, mix_norm_g[l], w_in[l], conv_w[l], conv_b[l], conv_ln_g[l], conv_ln_b[l],
                            conv_w_out[l], pool_w[l], pool_scale[l], sgu_ln_g[l], sgu_ln_b[l],
                            sgu_w_s[l], sgu_b_s[l], sgu_w_out[l], w_out[l])

    x = mixer(x, 0)
    x2 = _dense_ffn(x.reshape(b * s, d), ffn_norm_g[0], dense_w1[0], dense_w3[0], dense_w2[0])
    x = mixer(x2.reshape(b, s, d), 1)
    out = _moe_and_final_norm(x.reshape(b * s, d), ffn_norm_g[1], router_w[0], expert_w1[0],
                              expert_w3[0], expert_w2[0], final_norm_g)
    return out.reshape(b, s, d)
```

```python
import jax
import jax.numpy as jnp
from jax import lax
from jax.experimental import pallas as pl
from jax.experimental.pallas import tpu as pltpu

F32 = jnp.float32
BF16 = jnp.bfloat16

D_MODEL = 1024
DEPTH = 2
D_CONV = 512
CONV_WIDTH = 31
D_POOL = 512
POOL_WINDOWS = (2, 4, 8, 16)
N_POOL_GROUPS = 4
POOL_GROUP = D_POOL // N_POOL_GROUPS
POOL_OUT_GROUP = D_MODEL // N_POOL_GROUPS
D_SGU = 512
SGU_HEADS = 4
SGU_HEAD_DIM = D_SGU // SGU_HEADS
CHUNK = 128
SPLIT_A = 2 * D_CONV
SPLIT_B = SPLIT_A + D_POOL
SPLIT_U = SPLIT_B + D_SGU
SPLIT_V = SPLIT_U + D_SGU
GATE_A = SPLIT_V
GATE_B = GATE_A + D_MODEL
GATE_C = GATE_B + D_MODEL
D_IN = GATE_C + D_MODEL
D_FF = 2816
N_EXPERTS = 8
TOP_K = 2
D_EXPERT = 3584
MOE_BLOCK = 512
EPS = 1e-6

SUBLANES = 8
LANES = 128
LANE_TILES = D_MODEL // LANES
HALO = 32
SEQ_TILE = 512
CONV_ROWS = 32
PROJ_CHUNK = 256
FFN_TILE = 512
MOE_TILE = 512
EXPERT_CHUNK = 512
VMEM_LIMIT = 56 * 1024 * 1024
EXPERT_VMEM_LIMIT = 62 * 1024 * 1024

GROUP_SUBLANES = LANE_TILES * SUBLANES
TILE_GROUPS = MOE_TILE // SUBLANES
BLOCK_GROUPS = MOE_BLOCK // SUBLANES
LOCAL_GROUPS = TOP_K * TILE_GROUPS + N_EXPERTS
SEG_BITS = TILE_GROUPS.bit_length()
TILE_CHOICES = TOP_K * MOE_TILE
BLOCK_ROWS = BLOCK_GROUPS * GROUP_SUBLANES


def _rms(x, g):
    ms = jnp.mean(x * x, axis=-1, keepdims=True)
    return x * lax.rsqrt(ms + EPS) * g


def _layernorm(x, g, b):
    mu = jnp.mean(x, axis=-1, keepdims=True)
    xc = x - mu
    var = jnp.mean(xc * xc, axis=-1, keepdims=True)
    return xc * lax.rsqrt(var + EPS) * g + b


def _sigmoid(x):
    return 1.0 / (1.0 + jnp.exp(-x))


def _const_spec(shape):
    return pl.BlockSpec(shape, lambda *_: (0,) * len(shape), pipeline_mode=pl.Buffered(1))


def _to_groups(v):
    rows = v.shape[0]
    return [v[:, j * LANES:(j + 1) * LANES].reshape(rows // SUBLANES, SUBLANES, LANES)
            for j in range(LANE_TILES)]


def _store_groups(ref, v):
    for j, piece in enumerate(_to_groups(v)):
        ref[:, j * SUBLANES:(j + 1) * SUBLANES, :] = piece


def _load_groups(ref):
    rows = ref.shape[0] * SUBLANES
    return jnp.concatenate(
        [ref[:, j * SUBLANES:(j + 1) * SUBLANES, :].reshape(rows, LANES) for j in range(LANE_TILES)],
        axis=-1)


def _row_at(offset):
    return pl.ds(offset, LANE_TILES, stride=SUBLANES)


def _mixer_body(x_ref, ng_ref, win_ref, cw_ref, cb_ref, clg_ref, clb_ref, cwo_ref,
                pw_ref, ps_ref, slg_ref, slb_ref, sws_ref, sbs_ref, swo_ref, wo_ref,
                o_ref, abuf, ash, cbuf, pbuf, tbuf, usbuf, uvbuf, gbuf):
    ts = SEQ_TILE
    s = pl.program_id(1)

    @pl.when(s == 0)
    def _():
        abuf[0:HALO, :] = jnp.zeros((HALO, D_CONV), F32)
        pbuf[0:HALO, :] = jnp.zeros((HALO, D_POOL), F32)

    x = x_ref[0]
    h = _rms(x, ng_ref[...]).astype(BF16)

    def proj(lo, hi):
        return jnp.dot(h, win_ref[:, lo:hi], preferred_element_type=F32)

    def project_puv(c):
        lo = SPLIT_A + c * PROJ_CHUNK
        if lo < SPLIT_B:
            pbuf[HALO:HALO + ts, lo - SPLIT_A:lo - SPLIT_A + PROJ_CHUNK] = proj(lo, lo + PROJ_CHUNK)
        else:
            uvbuf[:, lo - SPLIT_B:lo - SPLIT_B + PROJ_CHUNK] = proj(lo, lo + PROJ_CHUNK)

    def project_gate(c):
        lo = c * PROJ_CHUNK
        gbuf[:, lo:lo + PROJ_CHUNK] = _sigmoid(proj(GATE_A + lo, GATE_A + lo + PROJ_CHUNK))

    n_puv = (SPLIT_V - SPLIT_A) // PROJ_CHUNK
    n_gate = (D_IN - GATE_A) // PROJ_CHUNK

    ain = proj(0, SPLIT_A)
    abuf[HALO:HALO + ts, :] = ain[:, :D_CONV] * _sigmoid(ain[:, D_CONV:])
    for r in range(1, SUBLANES):
        ash[r - 1] = abuf[r:r + ts + HALO - SUBLANES, :]
        if r - 1 < n_puv:
            project_puv(r - 1)
    for c in range(SUBLANES - 1, n_puv):
        project_puv(c)

    n_conv = ts // CONV_ROWS
    groups = CONV_ROWS // SUBLANES
    for ci in range(n_conv):
        r0 = ci * CONV_ROWS
        acc = jnp.broadcast_to(cb_ref[...][None], (groups, SUBLANES, D_CONV))
        for k in range(CONV_WIDTH):
            q, r = divmod(HALO - (CONV_WIDTH - 1) + k, SUBLANES)
            start = r0 + SUBLANES * q
            if r == 0:
                win = abuf[start:start + CONV_ROWS, :]
            else:
                win = ash[r - 1, start:start + CONV_ROWS, :]
            acc = acc + win.reshape(groups, SUBLANES, D_CONV) * cw_ref[k][None]
        cbuf[r0:r0 + CONV_ROWS, :] = acc.reshape(CONV_ROWS, D_CONV)
        for c in range(n_gate):
            if ((c + 1) * n_conv) // n_gate - 1 == ci:
                project_gate(c)
    abuf[0:HALO, :] = abuf[ts:ts + HALO, :]

    ca = _layernorm(cbuf[...], clg_ref[...], clb_ref[...])
    ca = (ca * _sigmoid(ca)).astype(BF16)
    y_a = jnp.dot(ca, cwo_ref[...], preferred_element_type=F32)
    mixed = gbuf[:, 0:D_MODEL] * y_a

    tbuf[0:SUBLANES, :] = jnp.zeros((SUBLANES, POOL_GROUP), F32)
    pos1 = (s * ts + 1 + lax.broadcasted_iota(jnp.int32, (ts, 1), 0))
    ext = ts + HALO
    yb_parts = []
    for g, w in enumerate(POOL_WINDOWS):
        lo = g * POOL_GROUP
        cur = pbuf[:, lo:lo + POOL_GROUP]
        sh = 1
        while sh < w:
            tbuf[SUBLANES:SUBLANES + ext, :] = cur
            cur = cur + tbuf[SUBLANES - sh:SUBLANES - sh + ext, :]
            sh *= 2
        cnt = jnp.minimum(pos1, w).astype(F32)
        pooled = cur[HALO:, :] / cnt - pbuf[HALO:HALO + ts, lo:lo + POOL_GROUP]
        yb_parts.append(jnp.dot(pooled.astype(BF16), pw_ref[g], preferred_element_type=F32))
    pbuf[0:HALO, :] = pbuf[ts:ts + HALO, :]
    y_b = jnp.concatenate(yb_parts, axis=-1) * ps_ref[...]
    mixed = mixed + gbuf[:, D_MODEL:2 * D_MODEL] * y_b

    vn = _layernorm(uvbuf[:, D_SGU:2 * D_SGU], slg_ref[...], slb_ref[...]).astype(BF16)
    row = lax.broadcasted_iota(jnp.int32, (CHUNK, CHUNK), 0)
    col = lax.broadcasted_iota(jnp.int32, (CHUNK, CHUNK), 1)
    for hh in range(SGU_HEADS):
        wm = jnp.where(col <= row, sws_ref[hh], 0.0).astype(BF16)
        c0 = hh * SGU_HEAD_DIM
        for c in range(ts // CHUNK):
            r0 = c * CHUNK
            sp = jnp.dot(wm, vn[r0:r0 + CHUNK, c0:c0 + SGU_HEAD_DIM], preferred_element_type=F32)
            sp = sp + sbs_ref[hh]
            usbuf[r0:r0 + CHUNK, c0:c0 + SGU_HEAD_DIM] = (
                uvbuf[r0:r0 + CHUNK, c0:c0 + SGU_HEAD_DIM] * sp).astype(BF16)
    y_c = jnp.dot(usbuf[...], swo_ref[...], preferred_element_type=F32)
    mixed = mixed + gbuf[:, 2 * D_MODEL:3 * D_MODEL] * y_c

    o_ref[0] = x + jnp.dot(mixed.astype(BF16), wo_ref[...], preferred_element_type=F32)


def _mixer_layer(x, ng, w_in, cw, cb, clg, clb, cwo, pw, ps, slg, slb, sws, sbs, swo, wo):
    b, s, d = x.shape
    ts = SEQ_TILE
    row = lambda v: v.reshape(1, -1)
    sbs_b = jnp.broadcast_to(sbs[:, :, None], (SGU_HEADS, CHUNK, SGU_HEAD_DIM))
    cw_b = jnp.broadcast_to(cw[:, None, :], (CONV_WIDTH, SUBLANES, D_CONV))
    cb_b = jnp.broadcast_to(cb[None, :], (SUBLANES, D_CONV))
    args = (x, row(ng), w_in.astype(BF16), cw_b, cb_b, row(clg), row(clb), cwo.astype(BF16),
            pw.astype(BF16), row(ps), row(slg), row(slb), sws, sbs_b, swo.astype(BF16), wo.astype(BF16))
    x_spec = pl.BlockSpec((1, ts, d), lambda bi, si: (bi, si, 0))
    in_specs = [x_spec] + [_const_spec(a.shape) for a in args[1:]]
    return pl.pallas_call(
        _mixer_body,
        grid=(b, s // ts),
        in_specs=in_specs,
        out_specs=x_spec,
        out_shape=jax.ShapeDtypeStruct(x.shape, F32),
        scratch_shapes=[
            pltpu.VMEM((ts + HALO, D_CONV), F32),
            pltpu.VMEM((SUBLANES - 1, ts + HALO - SUBLANES, D_CONV), F32),
            pltpu.VMEM((ts, D_CONV), F32),
            pltpu.VMEM((ts + HALO, D_POOL), F32),
            pltpu.VMEM((SUBLANES + ts + HALO, POOL_GROUP), F32),
            pltpu.VMEM((ts, D_SGU), BF16),
            pltpu.VMEM((ts, 2 * D_SGU), F32),
            pltpu.VMEM((ts, 3 * D_MODEL), F32),
        ],
        compiler_params=pltpu.CompilerParams(
            dimension_semantics=("arbitrary", "arbitrary"), vmem_limit_bytes=VMEM_LIMIT),
        name="token_mixer",
    )(*args)


def _dense_ffn_body(x_ref, g_ref, w1_ref, w3_ref, w2_ref, o_ref):
    x = x_ref[...]
    h = _rms(x, g_ref[...]).astype(BF16)
    a = jnp.dot(h, w1_ref[...], preferred_element_type=F32)
    b = jnp.dot(h, w3_ref[...], preferred_element_type=F32)
    act = (a * _sigmoid(a) * b).astype(BF16)
    o_ref[...] = x + jnp.dot(act, w2_ref[...], preferred_element_type=F32)


def _dense_ffn(x2, g, w1, w3, w2):
    t, d = x2.shape
    tm = FFN_TILE
    args = (x2, g.reshape(1, -1), w1.astype(BF16), w3.astype(BF16), w2.astype(BF16))
    x_spec = pl.BlockSpec((tm, d), lambda i: (i, 0))
    return pl.pallas_call(
        _dense_ffn_body,
        grid=(t // tm,),
        in_specs=[x_spec] + [_const_spec(a.shape) for a in args[1:]],
        out_specs=x_spec,
        out_shape=jax.ShapeDtypeStruct(x2.shape, F32),
        compiler_params=pltpu.CompilerParams(
            dimension_semantics=("arbitrary",), vmem_limit_bytes=VMEM_LIMIT),
        name="dense_ffn",
    )(*args)


def _router_body(x_ref, g_ref, wr_ref, tri_ref, q_ref, wt_ref, cnt_ref):
    tm = MOE_TILE
    h = _rms(x_ref[...], g_ref[...])
    w = wr_ref[...]
    h_hi = h.astype(BF16)
    h_lo = (h - h_hi.astype(F32)).astype(BF16)
    w_hi = w.astype(BF16)
    w_lo = (w - w_hi.astype(F32)).astype(BF16)

    def dot_nt(a, b):
        return lax.dot_general(a, b, (((1,), (1,)), ((), ())), preferred_element_type=F32)

    logits = dot_nt(w_hi, h_hi) + (dot_nt(w_hi, h_lo) + dot_nt(w_lo, h_hi))
    eid = lax.broadcasted_iota(jnp.int32, logits.shape, 0)
    m1 = jnp.max(logits, axis=0, keepdims=True)
    i1 = jnp.min(jnp.where(logits == m1, eid, N_EXPERTS), axis=0, keepdims=True)
    rest = jnp.where(eid == i1, -jnp.inf, logits)
    m2 = jnp.max(rest, axis=0, keepdims=True)
    i2 = jnp.min(jnp.where(rest == m2, eid, N_EXPERTS), axis=0, keepdims=True)
    e2 = jnp.exp(m2 - m1)
    den = 1.0 + e2
    wt_ref[0] = jnp.concatenate([1.0 / den, e2 / den], axis=1)

    oh1 = eid == i1
    oh2 = eid == i2
    onehot = jnp.where(oh1 | oh2, 1.0, 0.0)
    incl = jnp.dot(onehot.astype(BF16), tri_ref[...], preferred_element_type=F32)
    cnt = jnp.sum(onehot, axis=1, keepdims=True).astype(jnp.int32)
    seg = (lax.shift_right_logical(cnt + (SUBLANES - 1), 3) * SUBLANES).astype(F32)
    erow = lax.broadcasted_iota(jnp.int32, (N_EXPERTS, 1), 0)
    seg_start = jnp.zeros((N_EXPERTS, 1), F32)
    for e in range(N_EXPERTS - 1):
        seg_start = seg_start + jnp.where(erow > e, seg[e:e + 1, :], 0.0)
    base = seg_start + (incl - onehot)
    q1 = jnp.sum(jnp.where(oh1, base, 0.0), axis=0, keepdims=True)
    q2 = jnp.sum(jnp.where(oh2, base, 0.0), axis=0, keepdims=True)
    q = jnp.concatenate([q1, q2], axis=1).astype(jnp.int32)
    half = lax.rem(pl.program_id(0), 2) * (LOCAL_GROUPS * GROUP_SUBLANES)
    q_ref[0] = (lax.shift_right_logical(q, 3) * GROUP_SUBLANES + lax.bitwise_and(q, SUBLANES - 1)
                + half)
    cnt_ref[0] = jnp.broadcast_to(cnt, (N_EXPERTS, LANES))


def _router(x2, g, w_router):
    t, d = x2.shape
    tm = MOE_TILE
    nt = t // tm
    tri = jnp.triu(jnp.ones((tm, tm), BF16))
    tile_spec = pl.BlockSpec((1, 1, TOP_K * tm), lambda i: (i, 0, 0))
    return pl.pallas_call(
        _router_body,
        grid=(nt,),
        in_specs=[pl.BlockSpec((tm, d), lambda i: (i, 0)),
                  _const_spec((1, d)), _const_spec((N_EXPERTS, d)), _const_spec((tm, tm))],
        out_specs=[tile_spec, tile_spec, pl.BlockSpec((1, N_EXPERTS, LANES), lambda i: (i, 0, 0))],
        out_shape=[jax.ShapeDtypeStruct((nt, 1, TOP_K * tm), jnp.int32),
                   jax.ShapeDtypeStruct((nt, 1, TOP_K * tm), F32),
                   jax.ShapeDtypeStruct((nt, N_EXPERTS, LANES), jnp.int32)],
        compiler_params=pltpu.CompilerParams(dimension_semantics=("arbitrary",)),
        name="moe_router",
    )(x2, g.reshape(1, -1), w_router.T, tri)


def _segment_tables(cnt):
    seg = (cnt + (SUBLANES - 1)) // SUBLANES
    local_start = jnp.cumsum(seg, axis=1) - seg
    total = jnp.sum(seg, axis=0)
    padded = (total + BLOCK_GROUPS - 1) // BLOCK_GROUPS * BLOCK_GROUPS
    pend = jnp.cumsum(padded)
    pstart = pend - padded
    global_start = pstart[None, :] + jnp.cumsum(seg, axis=0) - seg
    nused = (pend[-1] // BLOCK_GROUPS).astype(jnp.int32).reshape(1)
    return dict(
        nseg=seg.reshape(-1).astype(jnp.int32),
        local_start=local_start.reshape(-1).astype(jnp.int32),
        global_start=global_start.reshape(-1).astype(jnp.int32),
        tail_start=(pstart + total).astype(jnp.int32),
        tail_len=(padded - total).astype(jnp.int32),
        pend=pend.astype(jnp.int32),
        nused=nused)


def _segment_pieces(nseg, local_start, global_start, tile, make_copy, act):
    for e in range(N_EXPERTS):
        k = tile * N_EXPERTS + e
        ng = nseg[k]
        l0 = local_start[k]
        g0 = global_start[k]
        for b in reversed(range(SEG_BITS)):
            size = 1 << b
            off = lax.shift_left(lax.shift_right_logical(ng, b + 1), b + 1)

            @pl.when(lax.bitwise_and(lax.shift_right_logical(ng, b), 1) == 1)
            def _():
                act(make_copy(l0 + off, g0 + off, size))


def _dispatch_body(ns_ref, ls_ref, gs_ref, ts_ref, tl_ref, nu_ref, q_hbm, x_ref, g_ref, xs_hbm,
                   q_smem, z, lbuf, zbuf, qsem, dsem, zsem):
    i = pl.program_id(0)
    n = pl.num_programs(0)
    slot = lax.rem(i, 2)

    def q_copy(tile, sl):
        return pltpu.make_async_copy(q_hbm.at[tile, 0], q_smem.at[pl.ds(sl * TILE_CHOICES, TILE_CHOICES)],
                                     qsem.at[sl])

    def seg_copy(sl):
        def make(loc, glob, size):
            return pltpu.make_async_copy(
                lbuf.at[pl.ds((sl * LOCAL_GROUPS + loc) * GROUP_SUBLANES, size * GROUP_SUBLANES)],
                xs_hbm.at[pl.ds(glob * GROUP_SUBLANES, size * GROUP_SUBLANES)], dsem.at[sl])
        return make

    @pl.when(i == 0)
    def _():
        q_copy(0, 0).start()
        lbuf[...] = jnp.zeros(lbuf.shape, F32)

    q_copy(i, slot).wait()

    @pl.when(i + 1 < n)
    def _():
        q_copy(i + 1, 1 - slot).start()

    _store_groups(z, _rms(x_ref[...], g_ref[...]))

    qbase = slot * TILE_CHOICES

    def scatter(ii, carry):
        for s in range(SUBLANES):
            t = ii * SUBLANES + s
            row = z[ii, pl.ds(s, LANE_TILES, stride=SUBLANES), :]
            for k in range(TOP_K):
                lbuf[_row_at(q_smem[qbase + k * MOE_TILE + t]), :] = row
        return carry

    lax.fori_loop(0, TILE_GROUPS, scatter, 0)

    @pl.when(i > 0)
    def _():
        _segment_pieces(ns_ref, ls_ref, gs_ref, i - 1, seg_copy(1 - slot), lambda c: c.wait())

    _segment_pieces(ns_ref, ls_ref, gs_ref, i, seg_copy(slot), lambda c: c.start())

    @pl.when(i == n - 1)
    def _():
        _segment_pieces(ns_ref, ls_ref, gs_ref, i, seg_copy(slot), lambda c: c.wait())
        zbuf[...] = jnp.zeros(zbuf.shape, F32)

        def tail_pieces(act):
            for e in range(N_EXPERTS):
                ng = tl_ref[e]
                g0 = ts_ref[e]
                for b in reversed(range(SEG_BITS)):
                    size = 1 << b
                    off = lax.shift_left(lax.shift_right_logical(ng, b + 1), b + 1)

                    @pl.when(lax.bitwise_and(lax.shift_right_logical(ng, b), 1) == 1)
                    def _():
                        act(pltpu.make_async_copy(
                            zbuf.at[pl.ds(0, size * GROUP_SUBLANES)],
                            xs_hbm.at[pl.ds((g0 + off) * GROUP_SUBLANES, size * GROUP_SUBLANES)],
                            zsem.at[0]))

        tail_pieces(lambda c: c.start())
        tail_pieces(lambda c: c.wait())

        def zero_block(blk, carry):
            c = pltpu.make_async_copy(zbuf, xs_hbm.at[pl.ds(blk * BLOCK_ROWS, BLOCK_ROWS)], zsem.at[0])
            c.start()
            c.wait()
            return carry

        lax.fori_loop(nu_ref[0], xs_hbm.shape[0] // BLOCK_ROWS, zero_block, 0)


def _dispatch(tabs, q, x2, g, n_groups):
    t, d = x2.shape
    tm = MOE_TILE
    nt = t // tm
    grid_spec = pltpu.PrefetchScalarGridSpec(
        num_scalar_prefetch=6,
        grid=(nt,),
        in_specs=[pl.BlockSpec(memory_space=pl.ANY),
                  pl.BlockSpec((tm, d), lambda i, *_: (i, 0)),
                  pl.BlockSpec((1, d), lambda i, *_: (0, 0), pipeline_mode=pl.Buffered(1))],
        out_specs=pl.BlockSpec(memory_space=pl.ANY),
        scratch_shapes=[
            pltpu.SMEM((2 * TILE_CHOICES,), jnp.int32),
            pltpu.VMEM((TILE_GROUPS, GROUP_SUBLANES, LANES), F32),
            pltpu.VMEM((2 * LOCAL_GROUPS * GROUP_SUBLANES, LANES), F32),
            pltpu.VMEM((BLOCK_ROWS, LANES), F32),
            pltpu.SemaphoreType.DMA((2,)),
            pltpu.SemaphoreType.DMA((2,)),
            pltpu.SemaphoreType.DMA((1,)),
        ],
    )
    return pl.pallas_call(
        _dispatch_body,
        grid_spec=grid_spec,
        out_shape=jax.ShapeDtypeStruct((n_groups * GROUP_SUBLANES, LANES), F32),
        compiler_params=pltpu.CompilerParams(
            dimension_semantics=("arbitrary",), vmem_limit_bytes=VMEM_LIMIT),
        name="moe_dispatch",
    )(tabs["nseg"], tabs["local_start"], tabs["global_start"], tabs["tail_start"], tabs["tail_len"],
      tabs["nused"], q, x2, g.reshape(1, -1))


def _expert_body(be_ref, nu_ref, xs_ref, w1_ref, w3_ref, w2_ref, o_ref, actbuf):
    i = pl.program_id(0)
    nused = nu_ref[0]

    @pl.when(i < nused)
    def _():
        h = _load_groups(xs_ref).astype(BF16)
        for j in range(D_EXPERT // EXPERT_CHUNK):
            cs = slice(j * EXPERT_CHUNK, (j + 1) * EXPERT_CHUNK)
            a = jnp.dot(h, w1_ref[0, :, cs], preferred_element_type=F32)
            b = jnp.dot(h, w3_ref[0, :, cs], preferred_element_type=F32)
            actbuf[:, cs] = (a * _sigmoid(a) * b).astype(BF16)
        _store_groups(o_ref, jnp.dot(actbuf[...], w2_ref[0], preferred_element_type=F32))

    @pl.when(i >= nused)
    def _():
        o_ref[...] = jnp.zeros(o_ref.shape, F32)


def _expert_ffn(block_e, nused, xs, w1, w3, w2):
    n_groups = xs.shape[0]
    nb = n_groups // BLOCK_GROUPS
    d = D_MODEL
    w13_spec = pl.BlockSpec((1, d, D_EXPERT), lambda i, be, nu: (be[i], 0, 0))
    w2_spec = pl.BlockSpec((1, D_EXPERT, d), lambda i, be, nu: (be[i], 0, 0))
    blk = (BLOCK_GROUPS, GROUP_SUBLANES, LANES)
    grid_spec = pltpu.PrefetchScalarGridSpec(
        num_scalar_prefetch=2,
        grid=(nb,),
        in_specs=[pl.BlockSpec(blk, lambda i, be, nu: (jnp.clip(i, 0, jnp.maximum(nu[0] - 1, 0)), 0, 0)),
                  w13_spec, w13_spec, w2_spec],
        out_specs=pl.BlockSpec(blk, lambda i, be, nu: (i, 0, 0)),
        scratch_shapes=[pltpu.VMEM((MOE_BLOCK, D_EXPERT), BF16)],
    )
    return pl.pallas_call(
        _expert_body,
        grid_spec=grid_spec,
        out_shape=jax.ShapeDtypeStruct(xs.shape, F32),
        compiler_params=pltpu.CompilerParams(
            dimension_semantics=("arbitrary",), vmem_limit_bytes=EXPERT_VMEM_LIMIT),
        name="moe_experts",
    )(block_e, nused, xs, w1.astype(BF16), w3.astype(BF16), w2.astype(BF16))


def _combine_body(ns_ref, ls_ref, gs_ref, q_hbm, w_hbm, y_hbm, x_ref, g_ref, o_ref,
                  q_smem, w_smem, ybuf, zo, qsem, wsem, ysem):
    i = pl.program_id(0)
    n = pl.num_programs(0)
    slot = lax.rem(i, 2)

    def meta_copies(tile, sl):
        dst = pl.ds(sl * TILE_CHOICES, TILE_CHOICES)
        return (pltpu.make_async_copy(q_hbm.at[tile, 0], q_smem.at[dst], qsem.at[sl]),
                pltpu.make_async_copy(w_hbm.at[tile, 0], w_smem.at[dst], wsem.at[sl]))

    def seg_copy(sl):
        def make(loc, glob, size):
            return pltpu.make_async_copy(
                y_hbm.at[pl.ds(glob * GROUP_SUBLANES, size * GROUP_SUBLANES)],
                ybuf.at[pl.ds((sl * LOCAL_GROUPS + loc) * GROUP_SUBLANES, size * GROUP_SUBLANES)],
                ysem.at[sl])
        return make

    def fetch(tile, sl):
        for c in meta_copies(tile, sl):
            c.start()
        _segment_pieces(ns_ref, ls_ref, gs_ref, tile, seg_copy(sl), lambda c: c.start())

    @pl.when(i == 0)
    def _():
        fetch(0, 0)

    @pl.when(i + 1 < n)
    def _():
        fetch(i + 1, 1 - slot)

    for c in meta_copies(i, slot):
        c.wait()
    _segment_pieces(ns_ref, ls_ref, gs_ref, i, seg_copy(slot), lambda c: c.wait())

    qbase = slot * TILE_CHOICES

    def gather(ii, carry):
        for s in range(SUBLANES):
            t0 = qbase + ii * SUBLANES + s
            t1 = t0 + MOE_TILE
            zo[ii, pl.ds(s, LANE_TILES, stride=SUBLANES), :] = (
                w_smem[t0] * ybuf[_row_at(q_smem[t0]), :] + w_smem[t1] * ybuf[_row_at(q_smem[t1]), :])
        return carry

    lax.fori_loop(0, TILE_GROUPS, gather, 0)
    o_ref[...] = _rms(x_ref[...] + _load_groups(zo), g_ref[...])


def _combine(tabs, q, wts, y, x2, g):
    t, d = x2.shape
    tm = MOE_TILE
    grid_spec = pltpu.PrefetchScalarGridSpec(
        num_scalar_prefetch=3,
        grid=(t // tm,),
        in_specs=[pl.BlockSpec(memory_space=pl.ANY),
                  pl.BlockSpec(memory_space=pl.ANY),
                  pl.BlockSpec(memory_space=pl.ANY),
                  pl.BlockSpec((tm, d), lambda i, *_: (i, 0)),
                  pl.BlockSpec((1, d), lambda i, *_: (0, 0), pipeline_mode=pl.Buffered(1))],
        out_specs=pl.BlockSpec((tm, d), lambda i, *_: (i, 0)),
        scratch_shapes=[
            pltpu.SMEM((2 * TILE_CHOICES,), jnp.int32),
            pltpu.SMEM((2 * TILE_CHOICES,), F32),
            pltpu.VMEM((2 * LOCAL_GROUPS * GROUP_SUBLANES, LANES), F32),
            pltpu.VMEM((TILE_GROUPS, GROUP_SUBLANES, LANES), F32),
            pltpu.SemaphoreType.DMA((2,)),
            pltpu.SemaphoreType.DMA((2,)),
            pltpu.SemaphoreType.DMA((2,)),
        ],
    )
    return pl.pallas_call(
        _combine_body,
        grid_spec=grid_spec,
        out_shape=jax.ShapeDtypeStruct(x2.shape, F32),
        compiler_params=pltpu.CompilerParams(
            dimension_semantics=("arbitrary",), vmem_limit_bytes=VMEM_LIMIT),
        name="moe_combine_norm",
    )(tabs["nseg"], tabs["local_start"], tabs["global_start"], q, wts, y, x2, g.reshape(1, -1))


def _moe_and_final_norm(x2, ffn_g, w_router, w1, w3, w2, final_g):
    t, d = x2.shape
    nt = t // MOE_TILE
    max_rows = TOP_K * t + nt * N_EXPERTS * (SUBLANES - 1) + N_EXPERTS * (MOE_BLOCK - 1)
    n_blocks = -(-max_rows // MOE_BLOCK)
    q, wts, cnt3 = _router(x2, ffn_g, w_router)
    tabs = _segment_tables(cnt3[:, :, 0])
    block_start = jnp.arange(n_blocks, dtype=jnp.int32) * BLOCK_GROUPS
    block_e = jnp.minimum(jnp.sum(block_start[:, None] >= tabs["pend"][None, :], axis=1),
                          N_EXPERTS - 1).astype(jnp.int32)
    xs = _dispatch(tabs, q, x2, ffn_g, n_blocks * BLOCK_GROUPS)
    y = _expert_ffn(block_e, tabs["nused"], xs.reshape(-1, GROUP_SUBLANES, LANES), w1, w3, w2)
    return _combine(tabs, q, wts, y.reshape(-1, LANES), x2, final_g)


def kernel(x, mix_norm_g, w_in, conv_w, conv_b, conv_ln_g, conv_ln_b, conv_w_out, pool_w, pool_scale,
           sgu_ln_g, sgu_ln_b, sgu_w_s, sgu_b_s, sgu_w_out, w_out, ffn_norm_g, dense_w1, dense_w3,
           dense_w2, router_w, expert_w1, expert_w3, expert_w2, final_norm_g):
    b, s, d = x.shape

    def mixer(xx, l):
        return _mixer_layer(xx, mix_norm_g[l], w_in[l], conv_w[l], conv_b[l], conv_ln_g[l], conv_ln_b[l],
                            conv_w_out[l], pool_w[l], pool_scale[l], sgu_ln_g[l], sgu_ln_b[l],
                            sgu_w_s[l], sgu_b_s[l], sgu_w_out[l], w_out[l])

    x = mixer(x, 0)
    x2 = _dense_ffn(x.reshape(b * s, d), ffn_norm_g[0], dense_w1[0], dense_w3[0], dense_w2[0])
    x = mixer(x2.reshape(b, s, d), 1)
    out = _moe_and_final_norm(x.reshape(b * s, d), ffn_norm_g[1], router_w[0], expert_w1[0],
                              expert_w3[0], expert_w2[0], final_norm_g)
    return out.reshape(b, s, d)
```

```python
import jax
import jax.numpy as jnp
from jax import lax
from jax.experimental import pallas as pl
from jax.experimental.pallas import tpu as pltpu

F32 = jnp.float32
BF16 = jnp.bfloat16

D_MODEL = 1024
DEPTH = 2
D_CONV = 512
CONV_WIDTH = 31
D_POOL = 512
POOL_WINDOWS = (2, 4, 8, 16)
N_POOL_GROUPS = 4
POOL_GROUP = D_POOL // N_POOL_GROUPS
POOL_OUT_GROUP = D_MODEL // N_POOL_GROUPS
D_SGU = 512
SGU_HEADS = 4
SGU_HEAD_DIM = D_SGU // SGU_HEADS
CHUNK = 128
SPLIT_A = 2 * D_CONV
SPLIT_B = SPLIT_A + D_POOL
SPLIT_U = SPLIT_B + D_SGU
SPLIT_V = SPLIT_U + D_SGU
GATE_A = SPLIT_V
GATE_B = GATE_A + D_MODEL
GATE_C = GATE_B + D_MODEL
D_IN = GATE_C + D_MODEL
D_FF = 2816
N_EXPERTS = 8
TOP_K = 2
D_EXPERT = 3584
MOE_BLOCK = 512
EPS = 1e-6

SUBLANES = 8
SUBLANE_SHIFT = 3
LANES = 128
LANE_TILES = D_MODEL // LANES
HALO = 32
SEQ_TILE = 512
CONV_ROWS = 32
PROJ_CHUNK = 256
FFN_TILE = 512
MOE_TILE = 1024
EXPERT_CHUNK = 512
VMEM_LIMIT = 56 * 1024 * 1024
EXPERT_VMEM_LIMIT = 62 * 1024 * 1024

GROUP_SUBLANES = LANE_TILES * SUBLANES
TILE_GROUPS = MOE_TILE // SUBLANES
BLOCK_GROUPS = MOE_BLOCK // SUBLANES
LOCAL_GROUPS = TOP_K * TILE_GROUPS + N_EXPERTS
SEG_BITS = TILE_GROUPS.bit_length()
TILE_CHOICES = TOP_K * MOE_TILE
BLOCK_ROWS = BLOCK_GROUPS * GROUP_SUBLANES


def _rms(x, g):
    ms = jnp.mean(x * x, axis=-1, keepdims=True)
    return x * lax.rsqrt(ms + EPS) * g


def _layernorm(x, g, b):
    mu = jnp.mean(x, axis=-1, keepdims=True)
    xc = x - mu
    var = jnp.mean(xc * xc, axis=-1, keepdims=True)
    return xc * lax.rsqrt(var + EPS) * g + b


def _sigmoid(x):
    return 1.0 / (1.0 + jnp.exp(-x))


def _const_spec(shape):
    return pl.BlockSpec(shape, lambda *_: (0,) * len(shape), pipeline_mode=pl.Buffered(1))


def _to_groups(v):
    rows = v.shape[0]
    return [v[:, j * LANES:(j + 1) * LANES].reshape(rows // SUBLANES, SUBLANES, LANES)
            for j in range(LANE_TILES)]


def _store_groups(ref, v):
    for j, piece in enumerate(_to_groups(v)):
        ref[:, j * SUBLANES:(j + 1) * SUBLANES, :] = piece


def _load_groups(ref):
    rows = ref.shape[0] * SUBLANES
    return jnp.concatenate(
        [ref[:, j * SUBLANES:(j + 1) * SUBLANES, :].reshape(rows, LANES) for j in range(LANE_TILES)],
        axis=-1)


def _row_at(offset):
    return pl.ds(offset, LANE_TILES, stride=SUBLANES)


def _mixer_body(x_ref, ng_ref, win_ref, cw_ref, cb_ref, clg_ref, clb_ref, cwo_ref,
                pw_ref, ps_ref, slg_ref, slb_ref, sws_ref, sbs_ref, swo_ref, wo_ref,
                o_ref, abuf, ash, cbuf, pbuf, tbuf, usbuf, uvbuf, gbuf):
    ts = SEQ_TILE
    s = pl.program_id(1)

    @pl.when(s == 0)
    def _():
        abuf[0:HALO, :] = jnp.zeros((HALO, D_CONV), F32)
        pbuf[0:HALO, :] = jnp.zeros((HALO, D_POOL), F32)

    x = x_ref[0]
    h = _rms(x, ng_ref[...]).astype(BF16)

    def proj(lo, hi):
        return jnp.dot(h, win_ref[:, lo:hi], preferred_element_type=F32)

    def project_puv(c):
        lo = SPLIT_A + c * PROJ_CHUNK
        if lo < SPLIT_B:
            pbuf[HALO:HALO + ts, lo - SPLIT_A:lo - SPLIT_A + PROJ_CHUNK] = proj(lo, lo + PROJ_CHUNK)
        else:
            uvbuf[:, lo - SPLIT_B:lo - SPLIT_B + PROJ_CHUNK] = proj(lo, lo + PROJ_CHUNK)

    def project_gate(c):
        lo = c * PROJ_CHUNK
        gbuf[:, lo:lo + PROJ_CHUNK] = _sigmoid(proj(GATE_A + lo, GATE_A + lo + PROJ_CHUNK))

    n_puv = (SPLIT_V - SPLIT_A) // PROJ_CHUNK
    n_gate = (D_IN - GATE_A) // PROJ_CHUNK

    ain = proj(0, SPLIT_A)
    abuf[HALO:HALO + ts, :] = ain[:, :D_CONV] * _sigmoid(ain[:, D_CONV:])
    for r in range(1, SUBLANES):
        ash[r - 1] = abuf[r:r + ts + HALO - SUBLANES, :]
        if r - 1 < n_puv:
            project_puv(r - 1)
    for c in range(SUBLANES - 1, n_puv):
        project_puv(c)

    n_conv = ts // CONV_ROWS
    groups = CONV_ROWS // SUBLANES
    for ci in range(n_conv):
        r0 = ci * CONV_ROWS
        acc = jnp.broadcast_to(cb_ref[...][None], (groups, SUBLANES, D_CONV))
        for k in range(CONV_WIDTH):
            q, r = divmod(HALO - (CONV_WIDTH - 1) + k, SUBLANES)
            start = r0 + SUBLANES * q
            if r == 0:
                win = abuf[start:start + CONV_ROWS, :]
            else:
                win = ash[r - 1, start:start + CONV_ROWS, :]
            acc = acc + win.reshape(groups, SUBLANES, D_CONV) * cw_ref[k][None]
        cbuf[r0:r0 + CONV_ROWS, :] = acc.reshape(CONV_ROWS, D_CONV)
        for c in range(n_gate):
            if ((c + 1) * n_conv) // n_gate - 1 == ci:
                project_gate(c)
    abuf[0:HALO, :] = abuf[ts:ts + HALO, :]

    ca = _layernorm(cbuf[...], clg_ref[...], clb_ref[...])
    ca = (ca * _sigmoid(ca)).astype(BF16)
    y_a = jnp.dot(ca, cwo_ref[...], preferred_element_type=F32)
    mixed = gbuf[:, 0:D_MODEL] * y_a

    tbuf[0:SUBLANES, :] = jnp.zeros((SUBLANES, POOL_GROUP), F32)
    pos1 = (s * ts + 1 + lax.broadcasted_iota(jnp.int32, (ts, 1), 0))
    ext = ts + HALO
    yb_parts = []
    for g, w in enumerate(POOL_WINDOWS):
        lo = g * POOL_GROUP
        cur = pbuf[:, lo:lo + POOL_GROUP]
        sh = 1
        while sh < w:
            tbuf[SUBLANES:SUBLANES + ext, :] = cur
            cur = cur + tbuf[SUBLANES - sh:SUBLANES - sh + ext, :]
            sh *= 2
        cnt = jnp.minimum(pos1, w).astype(F32)
        pooled = cur[HALO:, :] / cnt - pbuf[HALO:HALO + ts, lo:lo + POOL_GROUP]
        yb_parts.append(jnp.dot(pooled.astype(BF16), pw_ref[g], preferred_element_type=F32))
    pbuf[0:HALO, :] = pbuf[ts:ts + HALO, :]
    y_b = jnp.concatenate(yb_parts, axis=-1) * ps_ref[...]
    mixed = mixed + gbuf[:, D_MODEL:2 * D_MODEL] * y_b

    vn = _layernorm(uvbuf[:, D_SGU:2 * D_SGU], slg_ref[...], slb_ref[...]).astype(BF16)
    row = lax.broadcasted_iota(jnp.int32, (CHUNK, CHUNK), 0)
    col = lax.broadcasted_iota(jnp.int32, (CHUNK, CHUNK), 1)
    for hh in range(SGU_HEADS):
        wm = jnp.where(col <= row, sws_ref[hh], 0.0).astype(BF16)
        c0 = hh * SGU_HEAD_DIM
        for c in range(ts // CHUNK):
            r0 = c * CHUNK
            sp = jnp.dot(wm, vn[r0:r0 + CHUNK, c0:c0 + SGU_HEAD_DIM], preferred_element_type=F32)
            sp = sp + sbs_ref[hh]
            usbuf[r0:r0 + CHUNK, c0:c0 + SGU_HEAD_DIM] = (
                uvbuf[r0:r0 + CHUNK, c0:c0 + SGU_HEAD_DIM] * sp).astype(BF16)
    y_c = jnp.dot(usbuf[...], swo_ref[...], preferred_element_type=F32)
    mixed = mixed + gbuf[:, 2 * D_MODEL:3 * D_MODEL] * y_c

    o_ref[0] = x + jnp.dot(mixed.astype(BF16), wo_ref[...], preferred_element_type=F32)


def _mixer_layer(x, ng, w_in, cw, cb, clg, clb, cwo, pw, ps, slg, slb, sws, sbs, swo, wo):
    b, s, d = x.shape
    ts = SEQ_TILE
    row = lambda v: v.reshape(1, -1)
    sbs_b = jnp.broadcast_to(sbs[:, :, None], (SGU_HEADS, CHUNK, SGU_HEAD_DIM))
    cw_b = jnp.broadcast_to(cw[:, None, :], (CONV_WIDTH, SUBLANES, D_CONV))
    cb_b = jnp.broadcast_to(cb[None, :], (SUBLANES, D_CONV))
    args = (x, row(ng), w_in.astype(BF16), cw_b, cb_b, row(clg), row(clb), cwo.astype(BF16),
            pw.astype(BF16), row(ps), row(slg), row(slb), sws, sbs_b, swo.astype(BF16), wo.astype(BF16))
    x_spec = pl.BlockSpec((1, ts, d), lambda bi, si: (bi, si, 0))
    in_specs = [x_spec] + [_const_spec(a.shape) for a in args[1:]]
    return pl.pallas_call(
        _mixer_body,
        grid=(b, s // ts),
        in_specs=in_specs,
        out_specs=x_spec,
        out_shape=jax.ShapeDtypeStruct(x.shape, F32),
        scratch_shapes=[
            pltpu.VMEM((ts + HALO, D_CONV), F32),
            pltpu.VMEM((SUBLANES - 1, ts + HALO - SUBLANES, D_CONV), F32),
            pltpu.VMEM((ts, D_CONV), F32),
            pltpu.VMEM((ts + HALO, D_POOL), F32),
            pltpu.VMEM((SUBLANES + ts + HALO, POOL_GROUP), F32),
            pltpu.VMEM((ts, D_SGU), BF16),
            pltpu.VMEM((ts, 2 * D_SGU), F32),
            pltpu.VMEM((ts, 3 * D_MODEL), F32),
        ],
        compiler_params=pltpu.CompilerParams(
            dimension_semantics=("arbitrary", "arbitrary"), vmem_limit_bytes=VMEM_LIMIT),
        name="token_mixer",
    )(*args)


def _dense_ffn_body(x_ref, g_ref, w1_ref, w3_ref, w2_ref, o_ref):
    x = x_ref[...]
    h = _rms(x, g_ref[...]).astype(BF16)
    a = jnp.dot(h, w1_ref[...], preferred_element_type=F32)
    b = jnp.dot(h, w3_ref[...], preferred_element_type=F32)
    act = (a * _sigmoid(a) * b).astype(BF16)
    o_ref[...] = x + jnp.dot(act, w2_ref[...], preferred_element_type=F32)


def _dense_ffn(x2, g, w1, w3, w2):
    t, d = x2.shape
    tm = FFN_TILE
    args = (x2, g.reshape(1, -1), w1.astype(BF16), w3.astype(BF16), w2.astype(BF16))
    x_spec = pl.BlockSpec((tm, d), lambda i: (i, 0))
    return pl.pallas_call(
        _dense_ffn_body,
        grid=(t // tm,),
        in_specs=[x_spec] + [_const_spec(a.shape) for a in args[1:]],
        out_specs=x_spec,
        out_shape=jax.ShapeDtypeStruct(x2.shape, F32),
        compiler_params=pltpu.CompilerParams(
            dimension_semantics=("arbitrary",), vmem_limit_bytes=VMEM_LIMIT),
        name="dense_ffn",
    )(*args)


def _router_body(x_ref, g_ref, wr_ref, tri_ref, q_ref, wt_ref, cnt_ref):
    tm = MOE_TILE
    h = _rms(x_ref[...], g_ref[...])
    w = wr_ref[...]
    h_hi = h.astype(BF16)
    h_lo = (h - h_hi.astype(F32)).astype(BF16)
    w_hi = w.astype(BF16)
    w_lo = (w - w_hi.astype(F32)).astype(BF16)

    def dot_nt(a, b):
        return lax.dot_general(a, b, (((1,), (1,)), ((), ())), preferred_element_type=F32)

    logits = dot_nt(w_hi, h_hi) + (dot_nt(w_hi, h_lo) + dot_nt(w_lo, h_hi))
    eid = lax.broadcasted_iota(jnp.int32, logits.shape, 0)
    m1 = jnp.max(logits, axis=0, keepdims=True)
    i1 = jnp.min(jnp.where(logits == m1, eid, N_EXPERTS), axis=0, keepdims=True)
    rest = jnp.where(eid == i1, -jnp.inf, logits)
    m2 = jnp.max(rest, axis=0, keepdims=True)
    i2 = jnp.min(jnp.where(rest == m2, eid, N_EXPERTS), axis=0, keepdims=True)
    e2 = jnp.exp(m2 - m1)
    den = 1.0 + e2
    wt_ref[0] = jnp.concatenate([1.0 / den, e2 / den], axis=1)

    oh1 = eid == i1
    oh2 = eid == i2
    onehot = jnp.where(oh1 | oh2, 1.0, 0.0)
    incl = jnp.dot(onehot.astype(BF16), tri_ref[...], preferred_element_type=F32)
    cnt = jnp.sum(onehot, axis=1, keepdims=True).astype(jnp.int32)
    seg = (lax.shift_right_logical(cnt + (SUBLANES - 1), SUBLANE_SHIFT) * SUBLANES).astype(F32)
    erow = lax.broadcasted_iota(jnp.int32, (N_EXPERTS, 1), 0)
    seg_start = jnp.zeros((N_EXPERTS, 1), F32)
    for e in range(N_EXPERTS - 1):
        seg_start = seg_start + jnp.where(erow > e, seg[e:e + 1, :], 0.0)
    base = seg_start + (incl - onehot)
    q1 = jnp.sum(jnp.where(oh1, base, 0.0), axis=0, keepdims=True)
    q2 = jnp.sum(jnp.where(oh2, base, 0.0), axis=0, keepdims=True)
    q = jnp.concatenate([q1, q2], axis=1).astype(jnp.int32)
    half = lax.rem(pl.program_id(0), 2) * (LOCAL_GROUPS * GROUP_SUBLANES)
    q_ref[0] = (lax.shift_right_logical(q, SUBLANE_SHIFT) * GROUP_SUBLANES + lax.bitwise_and(q, SUBLANES - 1)
                + half)
    cnt_ref[0] = jnp.broadcast_to(cnt, (N_EXPERTS, LANES))


def _router(x2, g, w_router):
    t, d = x2.shape
    tm = MOE_TILE
    nt = t // tm
    tri = jnp.triu(jnp.ones((tm, tm), BF16))
    tile_spec = pl.BlockSpec((1, 1, TOP_K * tm), lambda i: (i, 0, 0))
    return pl.pallas_call(
        _router_body,
        grid=(nt,),
        in_specs=[pl.BlockSpec((tm, d), lambda i: (i, 0)),
                  _const_spec((1, d)), _const_spec((N_EXPERTS, d)), _const_spec((tm, tm))],
        out_specs=[tile_spec, tile_spec, pl.BlockSpec((1, N_EXPERTS, LANES), lambda i: (i, 0, 0))],
        out_shape=[jax.ShapeDtypeStruct((nt, 1, TOP_K * tm), jnp.int32),
                   jax.ShapeDtypeStruct((nt, 1, TOP_K * tm), F32),
                   jax.ShapeDtypeStruct((nt, N_EXPERTS, LANES), jnp.int32)],
        compiler_params=pltpu.CompilerParams(
            dimension_semantics=("arbitrary",), vmem_limit_bytes=VMEM_LIMIT),
        name="moe_router",
    )(x2, g.reshape(1, -1), w_router.T, tri)


def _segment_tables(cnt):
    seg = (cnt + (SUBLANES - 1)) // SUBLANES
    local_start = jnp.cumsum(seg, axis=1) - seg
    total = jnp.sum(seg, axis=0)
    padded = (total + BLOCK_GROUPS - 1) // BLOCK_GROUPS * BLOCK_GROUPS
    pend = jnp.cumsum(padded)
    pstart = pend - padded
    global_start = pstart[None, :] + jnp.cumsum(seg, axis=0) - seg
    nused = (pend[-1] // BLOCK_GROUPS).astype(jnp.int32).reshape(1)
    return dict(
        nseg=seg.reshape(-1).astype(jnp.int32),
        local_start=local_start.reshape(-1).astype(jnp.int32),
        global_start=global_start.reshape(-1).astype(jnp.int32),
        tail_start=(pstart + total).astype(jnp.int32),
        tail_len=(padded - total).astype(jnp.int32),
        pend=pend.astype(jnp.int32),
        nused=nused)


def _segment_pieces(nseg, local_start, global_start, tile, make_copy, act):
    for e in range(N_EXPERTS):
        k = tile * N_EXPERTS + e
        ng = nseg[k]
        l0 = local_start[k]
        g0 = global_start[k]
        for b in reversed(range(SEG_BITS)):
            size = 1 << b
            off = lax.shift_left(lax.shift_right_logical(ng, b + 1), b + 1)

            @pl.when(lax.bitwise_and(lax.shift_right_logical(ng, b), 1) == 1)
            def _():
                act(make_copy(l0 + off, g0 + off, size))


def _dispatch_body(ns_ref, ls_ref, gs_ref, ts_ref, tl_ref, nu_ref, q_hbm, x_ref, g_ref, xs_hbm,
                   q_smem, z, lbuf, zbuf, qsem, dsem, zsem):
    i = pl.program_id(0)
    n = pl.num_programs(0)
    slot = lax.rem(i, 2)

    def q_copy(tile, sl):
        return pltpu.make_async_copy(q_hbm.at[tile, 0], q_smem.at[pl.ds(sl * TILE_CHOICES, TILE_CHOICES)],
                                     qsem.at[sl])

    def seg_copy(sl):
        def make(loc, glob, size):
            return pltpu.make_async_copy(
                lbuf.at[pl.ds((sl * LOCAL_GROUPS + loc) * GROUP_SUBLANES, size * GROUP_SUBLANES)],
                xs_hbm.at[pl.ds(glob * GROUP_SUBLANES, size * GROUP_SUBLANES)], dsem.at[sl])
        return make

    @pl.when(i == 0)
    def _():
        q_copy(0, 0).start()
        lbuf[...] = jnp.zeros(lbuf.shape, F32)

    q_copy(i, slot).wait()

    @pl.when(i + 1 < n)
    def _():
        q_copy(i + 1, 1 - slot).start()

    _store_groups(z, _rms(x_ref[...], g_ref[...]))

    qbase = slot * TILE_CHOICES

    def scatter(ii, carry):
        for s in range(SUBLANES):
            t = ii * SUBLANES + s
            row = z[ii, pl.ds(s, LANE_TILES, stride=SUBLANES), :]
            for k in range(TOP_K):
                lbuf[_row_at(q_smem[qbase + k * MOE_TILE + t]), :] = row
        return carry

    lax.fori_loop(0, TILE_GROUPS, scatter, 0)

    @pl.when(i > 0)
    def _():
        _segment_pieces(ns_ref, ls_ref, gs_ref, i - 1, seg_copy(1 - slot), lambda c: c.wait())

    _segment_pieces(ns_ref, ls_ref, gs_ref, i, seg_copy(slot), lambda c: c.start())

    @pl.when(i == n - 1)
    def _():
        _segment_pieces(ns_ref, ls_ref, gs_ref, i, seg_copy(slot), lambda c: c.wait())
        zbuf[...] = jnp.zeros(zbuf.shape, F32)

        def tail_pieces(act):
            for e in range(N_EXPERTS):
                ng = tl_ref[e]
                g0 = ts_ref[e]
                for b in reversed(range(SEG_BITS)):
                    size = 1 << b
                    off = lax.shift_left(lax.shift_right_logical(ng, b + 1), b + 1)

                    @pl.when(lax.bitwise_and(lax.shift_right_logical(ng, b), 1) == 1)
                    def _():
                        act(pltpu.make_async_copy(
                            zbuf.at[pl.ds(0, size * GROUP_SUBLANES)],
                            xs_hbm.at[pl.ds((g0 + off) * GROUP_SUBLANES, size * GROUP_SUBLANES)],
                            zsem.at[0]))

        tail_pieces(lambda c: c.start())
        tail_pieces(lambda c: c.wait())

        def zero_block(blk, carry):
            c = pltpu.make_async_copy(zbuf, xs_hbm.at[pl.ds(blk * BLOCK_ROWS, BLOCK_ROWS)], zsem.at[0])
            c.start()
            c.wait()
            return carry

        lax.fori_loop(nu_ref[0], xs_hbm.shape[0] // BLOCK_ROWS, zero_block, 0)


def _dispatch(tabs, q, x2, g, n_groups):
    t, d = x2.shape
    tm = MOE_TILE
    nt = t // tm
    grid_spec = pltpu.PrefetchScalarGridSpec(
        num_scalar_prefetch=6,
        grid=(nt,),
        in_specs=[pl.BlockSpec(memory_space=pl.ANY),
                  pl.BlockSpec((tm, d), lambda i, *_: (i, 0)),
                  pl.BlockSpec((1, d), lambda i, *_: (0, 0), pipeline_mode=pl.Buffered(1))],
        out_specs=pl.BlockSpec(memory_space=pl.ANY),
        scratch_shapes=[
            pltpu.SMEM((2 * TILE_CHOICES,), jnp.int32),
            pltpu.VMEM((TILE_GROUPS, GROUP_SUBLANES, LANES), F32),
            pltpu.VMEM((2 * LOCAL_GROUPS * GROUP_SUBLANES, LANES), F32),
            pltpu.VMEM((BLOCK_ROWS, LANES), F32),
            pltpu.SemaphoreType.DMA((2,)),
            pltpu.SemaphoreType.DMA((2,)),
            pltpu.SemaphoreType.DMA((1,)),
        ],
    )
    return pl.pallas_call(
        _dispatch_body,
        grid_spec=grid_spec,
        out_shape=jax.ShapeDtypeStruct((n_groups * GROUP_SUBLANES, LANES), F32),
        compiler_params=pltpu.CompilerParams(
            dimension_semantics=("arbitrary",), vmem_limit_bytes=VMEM_LIMIT),
        name="moe_dispatch",
    )(tabs["nseg"], tabs["local_start"], tabs["global_start"], tabs["tail_start"], tabs["tail_len"],
      tabs["nused"], q, x2, g.reshape(1, -1))


def _expert_body(be_ref, nu_ref, xs_ref, w1_ref, w3_ref, w2_ref, o_ref, actbuf):
    i = pl.program_id(0)
    nused = nu_ref[0]

    @pl.when(i < nused)
    def _():
        h = _load_groups(xs_ref).astype(BF16)
        for j in range(D_EXPERT // EXPERT_CHUNK):
            cs = slice(j * EXPERT_CHUNK, (j + 1) * EXPERT_CHUNK)
            a = jnp.dot(h, w1_ref[0, :, cs], preferred_element_type=F32)
            b = jnp.dot(h, w3_ref[0, :, cs], preferred_element_type=F32)
            actbuf[:, cs] = (a * _sigmoid(a) * b).astype(BF16)
        _store_groups(o_ref, jnp.dot(actbuf[...], w2_ref[0], preferred_element_type=F32))

    @pl.when(i >= nused)
    def _():
        o_ref[...] = jnp.zeros(o_ref.shape, F32)


def _expert_ffn(block_e, nused, xs, w1, w3, w2):
    n_groups = xs.shape[0]
    nb = n_groups // BLOCK_GROUPS
    d = D_MODEL
    w13_spec = pl.BlockSpec((1, d, D_EXPERT), lambda i, be, nu: (be[i], 0, 0))
    w2_spec = pl.BlockSpec((1, D_EXPERT, d), lambda i, be, nu: (be[i], 0, 0))
    blk = (BLOCK_GROUPS, GROUP_SUBLANES, LANES)
    grid_spec = pltpu.PrefetchScalarGridSpec(
        num_scalar_prefetch=2,
        grid=(nb,),
        in_specs=[pl.BlockSpec(blk, lambda i, be, nu: (jnp.clip(i, 0, jnp.maximum(nu[0] - 1, 0)), 0, 0)),
                  w13_spec, w13_spec, w2_spec],
        out_specs=pl.BlockSpec(blk, lambda i, be, nu: (i, 0, 0)),
        scratch_shapes=[pltpu.VMEM((MOE_BLOCK, D_EXPERT), BF16)],
    )
    return pl.pallas_call(
        _expert_body,
        grid_spec=grid_spec,
        out_shape=jax.ShapeDtypeStruct(xs.shape, F32),
        compiler_params=pltpu.CompilerParams(
            dimension_semantics=("arbitrary",), vmem_limit_bytes=EXPERT_VMEM_LIMIT),
        name="moe_experts",
    )(block_e, nused, xs, w1.astype(BF16), w3.astype(BF16), w2.astype(BF16))


def _combine_body(ns_ref, ls_ref, gs_ref, q_hbm, w_hbm, y_hbm, x_ref, g_ref, o_ref,
                  q_smem, w_smem, ybuf, zo, qsem, wsem, ysem):
    i = pl.program_id(0)
    n = pl.num_programs(0)
    slot = lax.rem(i, 2)

    def meta_copies(tile, sl):
        dst = pl.ds(sl * TILE_CHOICES, TILE_CHOICES)
        return (pltpu.make_async_copy(q_hbm.at[tile, 0], q_smem.at[dst], qsem.at[sl]),
                pltpu.make_async_copy(w_hbm.at[tile, 0], w_smem.at[dst], wsem.at[sl]))

    def seg_copy(sl):
        def make(loc, glob, size):
            return pltpu.make_async_copy(
                y_hbm.at[pl.ds(glob * GROUP_SUBLANES, size * GROUP_SUBLANES)],
                ybuf.at[pl.ds((sl * LOCAL_GROUPS + loc) * GROUP_SUBLANES, size * GROUP_SUBLANES)],
                ysem.at[sl])
        return make

    def fetch(tile, sl):
        for c in meta_copies(tile, sl):
            c.start()
        _segment_pieces(ns_ref, ls_ref, gs_ref, tile, seg_copy(sl), lambda c: c.start())

    @pl.when(i == 0)
    def _():
        fetch(0, 0)

    @pl.when(i + 1 < n)
    def _():
        fetch(i + 1, 1 - slot)

    for c in meta_copies(i, slot):
        c.wait()
    _segment_pieces(ns_ref, ls_ref, gs_ref, i, seg_copy(slot), lambda c: c.wait())

    qbase = slot * TILE_CHOICES

    def gather(ii, carry):
        for s in range(SUBLANES):
            t0 = qbase + ii * SUBLANES + s
            t1 = t0 + MOE_TILE
            zo[ii, pl.ds(s, LANE_TILES, stride=SUBLANES), :] = (
                w_smem[t0] * ybuf[_row_at(q_smem[t0]), :] + w_smem[t1] * ybuf[_row_at(q_smem[t1]), :])
        return carry

    lax.fori_loop(0, TILE_GROUPS, gather, 0)
    o_ref[...] = _rms(x_ref[...] + _load_groups(zo), g_ref[...])


def _combine(tabs, q, wts, y, x2, g):
    t, d = x2.shape
    tm = MOE_TILE
    grid_spec = pltpu.PrefetchScalarGridSpec(
        num_scalar_prefetch=3,
        grid=(t // tm,),
        in_specs=[pl.BlockSpec(memory_space=pl.ANY),
                  pl.BlockSpec(memory_space=pl.ANY),
                  pl.BlockSpec(memory_space=pl.ANY),
                  pl.BlockSpec((tm, d), lambda i, *_: (i, 0)),
                  pl.BlockSpec((1, d), lambda i, *_: (0, 0), pipeline_mode=pl.Buffered(1))],
        out_specs=pl.BlockSpec((tm, d), lambda i, *_: (i, 0)),
        scratch_shapes=[
            pltpu.SMEM((2 * TILE_CHOICES,), jnp.int32),
            pltpu.SMEM((2 * TILE_CHOICES,), F32),
            pltpu.VMEM((2 * LOCAL_GROUPS * GROUP_SUBLANES, LANES), F32),
            pltpu.VMEM((TILE_GROUPS, GROUP_SUBLANES, LANES), F32),
            pltpu.SemaphoreType.DMA((2,)),
            pltpu.SemaphoreType.DMA((2,)),
            pltpu.SemaphoreType.DMA((2,)),
        ],
    )
    return pl.pallas_call(
        _combine_body,
        grid_spec=grid_spec,
        out_shape=jax.ShapeDtypeStruct(x2.shape, F32),
        compiler_params=pltpu.CompilerParams(
            dimension_semantics=("arbitrary",), vmem_limit_bytes=VMEM_LIMIT),
        name="moe_combine_norm",
    )(tabs["nseg"], tabs["local_start"], tabs["global_start"], q, wts, y, x2, g.reshape(1, -1))


def _moe_and_final_norm(x2, ffn_g, w_router, w1, w3, w2, final_g):
    t, d = x2.shape
    nt = t // MOE_TILE
    max_rows = TOP_K * t + nt * N_EXPERTS * (SUBLANES - 1) + N_EXPERTS * (MOE_BLOCK - 1)
    n_blocks = -(-max_rows // MOE_BLOCK)
    q, wts, cnt3 = _router(x2, ffn_g, w_router)
    tabs = _segment_tables(cnt3[:, :, 0])
    block_start = jnp.arange(n_blocks, dtype=jnp.int32) * BLOCK_GROUPS
    block_e = jnp.minimum(jnp.sum(block_start[:, None] >= tabs["pend"][None, :], axis=1),
                          N_EXPERTS - 1).astype(jnp.int32)
    xs = _dispatch(tabs, q, x2, ffn_g, n_blocks * BLOCK_GROUPS)
    y = _expert_ffn(block_e, tabs["nused"], xs.reshape(-1, GROUP_SUBLANES, LANES), w1, w3, w2)
    return _combine(tabs, q, wts, y.reshape(-1, LANES), x2, final_g)


def kernel(x, mix_norm_g, w_in, conv_w, conv_b, conv_ln_g, conv_ln_b, conv_w_out, pool_w, pool_scale,
           sgu_ln_g, sgu_ln_b, sgu_w_s, sgu_b_s, sgu_w_out, w_out, ffn_norm_g, dense_w1, dense_w3,
           dense_w2, router_w, expert_w1, expert_w3, expert_w2, final_norm_g):
    b, s, d = x.shape

    def mixer(xx, l):
        return _mixer_layer(xx, mix_norm_g[l], w_in[l], conv_w[l], conv_b[l], conv_ln_g[l], conv_ln_b[l],
                            conv_w_out[l], pool_w[l], pool_scale[l], sgu_ln_g[l], sgu_ln_b[l],
                            sgu_w_s[l], sgu_b_s[l], sgu_w_out[l], w_out[l])

    x = mixer(x, 0)
    x2 = _dense_ffn(x.reshape(b * s, d), ffn_norm_g[0], dense_w1[0], dense_w3[0], dense_w2[0])
    x = mixer(x2.reshape(b, s, d), 1)
    out = _moe_and_final_norm(x.reshape(b * s, d), ffn_norm_g[1], router_w[0], expert_w1[0],
                              expert_w3[0], expert_w2[0], final_norm_g)
    return out.reshape(b, s, d)
```

```python
import jax
import jax.numpy as jnp
from jax import lax
from jax.experimental import pallas as pl
from jax.experimental.pallas import tpu as pltpu

F32 = jnp.float32
BF16 = jnp.bfloat16

D_MODEL = 1024
DEPTH = 2
D_CONV = 512
CONV_WIDTH = 31
D_POOL = 512
POOL_WINDOWS = (2, 4, 8, 16)
N_POOL_GROUPS = 4
POOL_GROUP = D_POOL // N_POOL_GROUPS
POOL_OUT_GROUP = D_MODEL // N_POOL_GROUPS
D_SGU = 512
SGU_HEADS = 4
SGU_HEAD_DIM = D_SGU // SGU_HEADS
CHUNK = 128
SPLIT_A = 2 * D_CONV
SPLIT_B = SPLIT_A + D_POOL
SPLIT_U = SPLIT_B + D_SGU
SPLIT_V = SPLIT_U + D_SGU
GATE_A = SPLIT_V
GATE_B = GATE_A + D_MODEL
GATE_C = GATE_B + D_MODEL
D_IN = GATE_C + D_MODEL
D_FF = 2816
N_EXPERTS = 8
TOP_K = 2
D_EXPERT = 3584
MOE_BLOCK = 512
EPS = 1e-6

SUBLANES = 8
SUBLANE_SHIFT = 3
LANES = 128
LANE_TILES = D_MODEL // LANES
HALO = 32
SEQ_TILE = 512
CONV_ROWS = 32
PROJ_CHUNK = 256
FFN_TILE = 512
MOE_TILE = 1024
EXPERT_CHUNK = 512
VMEM_LIMIT = 56 * 1024 * 1024
EXPERT_VMEM_LIMIT = 62 * 1024 * 1024

GROUP_SUBLANES = LANE_TILES * SUBLANES
TILE_GROUPS = MOE_TILE // SUBLANES
BLOCK_GROUPS = MOE_BLOCK // SUBLANES
LOCAL_GROUPS = TOP_K * TILE_GROUPS + N_EXPERTS
SEG_BITS = TILE_GROUPS.bit_length()
TILE_CHOICES = TOP_K * MOE_TILE
BLOCK_ROWS = BLOCK_GROUPS * GROUP_SUBLANES


def _rms(x, g):
    ms = jnp.mean(x * x, axis=-1, keepdims=True)
    return x * lax.rsqrt(ms + EPS) * g


def _layernorm(x, g, b):
    mu = jnp.mean(x, axis=-1, keepdims=True)
    xc = x - mu
    var = jnp.mean(xc * xc, axis=-1, keepdims=True)
    return xc * lax.rsqrt(var + EPS) * g + b


def _sigmoid(x):
    return 1.0 / (1.0 + jnp.exp(-x))


def _const_spec(shape):
    return pl.BlockSpec(shape, lambda *_: (0,) * len(shape), pipeline_mode=pl.Buffered(1))


def _to_groups(v):
    rows = v.shape[0]
    return [v[:, j * LANES:(j + 1) * LANES].reshape(rows // SUBLANES, SUBLANES, LANES)
            for j in range(LANE_TILES)]


def _store_groups(ref, v):
    for j, piece in enumerate(_to_groups(v)):
        ref[:, j * SUBLANES:(j + 1) * SUBLANES, :] = piece


def _load_groups(ref):
    rows = ref.shape[0] * SUBLANES
    return jnp.concatenate(
        [ref[:, j * SUBLANES:(j + 1) * SUBLANES, :].reshape(rows, LANES) for j in range(LANE_TILES)],
        axis=-1)


def _row_at(offset):
    return pl.ds(offset, LANE_TILES, stride=SUBLANES)


def _mixer_body(x_ref, ng_ref, win_ref, cw_ref, cb_ref, clg_ref, clb_ref, cwo_ref,
                pw_ref, ps_ref, slg_ref, slb_ref, sws_ref, sbs_ref, swo_ref, wo_ref,
                o_ref, abuf, cbuf, pbuf, tbuf, usbuf, uvbuf, gbuf):
    ts = SEQ_TILE
    s = pl.program_id(1)

    @pl.when(s == 0)
    def _():
        abuf[:, 0:HALO, :] = jnp.zeros((D_CONV // LANES, HALO, LANES), F32)
        pbuf[0:HALO, :] = jnp.zeros((HALO, D_POOL), F32)

    x = x_ref[0]
    h = _rms(x, ng_ref[...]).astype(BF16)

    def proj(lo, hi):
        return jnp.dot(h, win_ref[:, lo:hi], preferred_element_type=F32)

    def project_puv(c):
        lo = SPLIT_A + c * PROJ_CHUNK
        if lo < SPLIT_B:
            pbuf[HALO:HALO + ts, lo - SPLIT_A:lo - SPLIT_A + PROJ_CHUNK] = proj(lo, lo + PROJ_CHUNK)
        else:
            uvbuf[:, lo - SPLIT_B:lo - SPLIT_B + PROJ_CHUNK] = proj(lo, lo + PROJ_CHUNK)

    def project_gate(c):
        lo = c * PROJ_CHUNK
        gbuf[:, lo:lo + PROJ_CHUNK] = _sigmoid(proj(GATE_A + lo, GATE_A + lo + PROJ_CHUNK))

    n_puv = (SPLIT_V - SPLIT_A) // PROJ_CHUNK
    n_gate = (D_IN - GATE_A) // PROJ_CHUNK

    ain = proj(0, SPLIT_A)
    a = ain[:, :D_CONV] * _sigmoid(ain[:, D_CONV:])
    n_slab = D_CONV // LANES
    for lt in range(n_slab):
        abuf[lt, HALO:HALO + ts, :] = a[:, lt * LANES:(lt + 1) * LANES]

    pieces = ([(project_puv, c) for c in range(n_puv)] + [(project_gate, c) for c in range(n_gate)])
    chunks = [(lt, r0) for lt in range(n_slab) for r0 in range(0, ts, CONV_ROWS)]
    groups = CONV_ROWS // SUBLANES
    first_tap = HALO - (CONV_WIDTH - 1)
    for ci, (lt, r0) in enumerate(chunks):
        ls = slice(lt * LANES, (lt + 1) * LANES)
        acc = jnp.broadcast_to(cb_ref[:, ls][None], (groups, SUBLANES, LANES))
        for k in range(CONV_WIDTH):
            win = abuf[lt, r0 + first_tap + k:r0 + first_tap + k + CONV_ROWS, :]
            acc = acc + win.reshape(groups, SUBLANES, LANES) * cw_ref[k, :, ls][None]
        cbuf[r0:r0 + CONV_ROWS, ls] = acc.reshape(CONV_ROWS, LANES)
        for pi, (project, c) in enumerate(pieces):
            if (pi * len(chunks)) // len(pieces) == ci:
                project(c)
    abuf[:, 0:HALO, :] = abuf[:, ts:ts + HALO, :]

    ca = _layernorm(cbuf[...], clg_ref[...], clb_ref[...])
    ca = (ca * _sigmoid(ca)).astype(BF16)
    y_a = jnp.dot(ca, cwo_ref[...], preferred_element_type=F32)
    mixed = gbuf[:, 0:D_MODEL] * y_a

    tbuf[0:SUBLANES, :] = jnp.zeros((SUBLANES, POOL_GROUP), F32)
    pos1 = (s * ts + 1 + lax.broadcasted_iota(jnp.int32, (ts, 1), 0))
    ext = ts + HALO
    yb_parts = []
    for g, w in enumerate(POOL_WINDOWS):
        lo = g * POOL_GROUP
        cur = pbuf[:, lo:lo + POOL_GROUP]
        sh = 1
        while sh < w:
            tbuf[SUBLANES:SUBLANES + ext, :] = cur
            cur = cur + tbuf[SUBLANES - sh:SUBLANES - sh + ext, :]
            sh *= 2
        cnt = jnp.minimum(pos1, w).astype(F32)
        pooled = cur[HALO:, :] / cnt - pbuf[HALO:HALO + ts, lo:lo + POOL_GROUP]
        yb_parts.append(jnp.dot(pooled.astype(BF16), pw_ref[g], preferred_element_type=F32))
    pbuf[0:HALO, :] = pbuf[ts:ts + HALO, :]
    y_b = jnp.concatenate(yb_parts, axis=-1) * ps_ref[...]
    mixed = mixed + gbuf[:, D_MODEL:2 * D_MODEL] * y_b

    vn = _layernorm(uvbuf[:, D_SGU:2 * D_SGU], slg_ref[...], slb_ref[...]).astype(BF16)
    row = lax.broadcasted_iota(jnp.int32, (CHUNK, CHUNK), 0)
    col = lax.broadcasted_iota(jnp.int32, (CHUNK, CHUNK), 1)
    for hh in range(SGU_HEADS):
        wm = jnp.where(col <= row, sws_ref[hh], 0.0).astype(BF16)
        c0 = hh * SGU_HEAD_DIM
        for c in range(ts // CHUNK):
            r0 = c * CHUNK
            sp = jnp.dot(wm, vn[r0:r0 + CHUNK, c0:c0 + SGU_HEAD_DIM], preferred_element_type=F32)
            sp = sp + sbs_ref[hh]
            usbuf[r0:r0 + CHUNK, c0:c0 + SGU_HEAD_DIM] = (
                uvbuf[r0:r0 + CHUNK, c0:c0 + SGU_HEAD_DIM] * sp).astype(BF16)
    y_c = jnp.dot(usbuf[...], swo_ref[...], preferred_element_type=F32)
    mixed = mixed + gbuf[:, 2 * D_MODEL:3 * D_MODEL] * y_c

    o_ref[0] = x + jnp.dot(mixed.astype(BF16), wo_ref[...], preferred_element_type=F32)


def _mixer_layer(x, ng, w_in, cw, cb, clg, clb, cwo, pw, ps, slg, slb, sws, sbs, swo, wo):
    b, s, d = x.shape
    ts = SEQ_TILE
    row = lambda v: v.reshape(1, -1)
    sbs_b = jnp.broadcast_to(sbs[:, :, None], (SGU_HEADS, CHUNK, SGU_HEAD_DIM))
    cw_b = jnp.broadcast_to(cw[:, None, :], (CONV_WIDTH, SUBLANES, D_CONV))
    cb_b = jnp.broadcast_to(cb[None, :], (SUBLANES, D_CONV))
    args = (x, row(ng), w_in.astype(BF16), cw_b, cb_b, row(clg), row(clb), cwo.astype(BF16),
            pw.astype(BF16), row(ps), row(slg), row(slb), sws, sbs_b, swo.astype(BF16), wo.astype(BF16))
    x_spec = pl.BlockSpec((1, ts, d), lambda bi, si: (bi, si, 0))
    in_specs = [x_spec] + [_const_spec(a.shape) for a in args[1:]]
    return pl.pallas_call(
        _mixer_body,
        grid=(b, s // ts),
        in_specs=in_specs,
        out_specs=x_spec,
        out_shape=jax.ShapeDtypeStruct(x.shape, F32),
        scratch_shapes=[
            pltpu.VMEM((D_CONV // LANES, ts + HALO, LANES), F32),
            pltpu.VMEM((ts, D_CONV), F32),
            pltpu.VMEM((ts + HALO, D_POOL), F32),
            pltpu.VMEM((SUBLANES + ts + HALO, POOL_GROUP), F32),
            pltpu.VMEM((ts, D_SGU), BF16),
            pltpu.VMEM((ts, 2 * D_SGU), F32),
            pltpu.VMEM((ts, 3 * D_MODEL), F32),
        ],
        compiler_params=pltpu.CompilerParams(
            dimension_semantics=("arbitrary", "arbitrary"), vmem_limit_bytes=VMEM_LIMIT),
        name="token_mixer",
    )(*args)


def _dense_ffn_body(x_ref, g_ref, w1_ref, w3_ref, w2_ref, o_ref):
    x = x_ref[...]
    h = _rms(x, g_ref[...]).astype(BF16)
    a = jnp.dot(h, w1_ref[...], preferred_element_type=F32)
    b = jnp.dot(h, w3_ref[...], preferred_element_type=F32)
    act = (a * _sigmoid(a) * b).astype(BF16)
    o_ref[...] = x + jnp.dot(act, w2_ref[...], preferred_element_type=F32)


def _dense_ffn(x2, g, w1, w3, w2):
    t, d = x2.shape
    tm = FFN_TILE
    args = (x2, g.reshape(1, -1), w1.astype(BF16), w3.astype(BF16), w2.astype(BF16))
    x_spec = pl.BlockSpec((tm, d), lambda i: (i, 0))
    return pl.pallas_call(
        _dense_ffn_body,
        grid=(t // tm,),
        in_specs=[x_spec] + [_const_spec(a.shape) for a in args[1:]],
        out_specs=x_spec,
        out_shape=jax.ShapeDtypeStruct(x2.shape, F32),
        compiler_params=pltpu.CompilerParams(
            dimension_semantics=("arbitrary",), vmem_limit_bytes=VMEM_LIMIT),
        name="dense_ffn",
    )(*args)


def _router_body(x_ref, g_ref, wr_ref, tri_ref, q_ref, wt_ref, cnt_ref):
    tm = MOE_TILE
    h = _rms(x_ref[...], g_ref[...])
    w = wr_ref[...]
    h_hi = h.astype(BF16)
    h_lo = (h - h_hi.astype(F32)).astype(BF16)
    w_hi = w.astype(BF16)
    w_lo = (w - w_hi.astype(F32)).astype(BF16)

    def dot_nt(a, b):
        return lax.dot_general(a, b, (((1,), (1,)), ((), ())), preferred_element_type=F32)

    logits = dot_nt(w_hi, h_hi) + (dot_nt(w_hi, h_lo) + dot_nt(w_lo, h_hi))
    eid = lax.broadcasted_iota(jnp.int32, logits.shape, 0)
    m1 = jnp.max(logits, axis=0, keepdims=True)
    i1 = jnp.min(jnp.where(logits == m1, eid, N_EXPERTS), axis=0, keepdims=True)
    rest = jnp.where(eid == i1, -jnp.inf, logits)
    m2 = jnp.max(rest, axis=0, keepdims=True)
    i2 = jnp.min(jnp.where(rest == m2, eid, N_EXPERTS), axis=0, keepdims=True)
    e2 = jnp.exp(m2 - m1)
    den = 1.0 + e2
    wt_ref[0] = jnp.concatenate([1.0 / den, e2 / den], axis=1)

    oh1 = eid == i1
    oh2 = eid == i2
    onehot = jnp.where(oh1 | oh2, 1.0, 0.0)
    incl = jnp.dot(onehot.astype(BF16), tri_ref[...], preferred_element_type=F32)
    cnt = jnp.sum(onehot, axis=1, keepdims=True).astype(jnp.int32)
    seg = (lax.shift_right_logical(cnt + (SUBLANES - 1), SUBLANE_SHIFT) * SUBLANES).astype(F32)
    erow = lax.broadcasted_iota(jnp.int32, (N_EXPERTS, 1), 0)
    seg_start = jnp.zeros((N_EXPERTS, 1), F32)
    for e in range(N_EXPERTS - 1):
        seg_start = seg_start + jnp.where(erow > e, seg[e:e + 1, :], 0.0)
    base = seg_start + (incl - onehot)
    q1 = jnp.sum(jnp.where(oh1, base, 0.0), axis=0, keepdims=True)
    q2 = jnp.sum(jnp.where(oh2, base, 0.0), axis=0, keepdims=True)
    q = jnp.concatenate([q1, q2], axis=1).astype(jnp.int32)
    half = lax.rem(pl.program_id(0), 2) * (LOCAL_GROUPS * GROUP_SUBLANES)
    q_ref[0] = (lax.shift_right_logical(q, SUBLANE_SHIFT) * GROUP_SUBLANES + lax.bitwise_and(q, SUBLANES - 1)
                + half)
    cnt_ref[0] = jnp.broadcast_to(cnt, (N_EXPERTS, LANES))


def _router(x2, g, w_router):
    t, d = x2.shape
    tm = MOE_TILE
    nt = t // tm
    tri = jnp.triu(jnp.ones((tm, tm), BF16))
    tile_spec = pl.BlockSpec((1, 1, TOP_K * tm), lambda i: (i, 0, 0))
    return pl.pallas_call(
        _router_body,
        grid=(nt,),
        in_specs=[pl.BlockSpec((tm, d), lambda i: (i, 0)),
                  _const_spec((1, d)), _const_spec((N_EXPERTS, d)), _const_spec((tm, tm))],
        out_specs=[tile_spec, tile_spec, pl.BlockSpec((1, N_EXPERTS, LANES), lambda i: (i, 0, 0))],
        out_shape=[jax.ShapeDtypeStruct((nt, 1, TOP_K * tm), jnp.int32),
                   jax.ShapeDtypeStruct((nt, 1, TOP_K * tm), F32),
                   jax.ShapeDtypeStruct((nt, N_EXPERTS, LANES), jnp.int32)],
        compiler_params=pltpu.CompilerParams(
            dimension_semantics=("arbitrary",), vmem_limit_bytes=VMEM_LIMIT),
        name="moe_router",
    )(x2, g.reshape(1, -1), w_router.T, tri)


def _segment_tables(cnt):
    seg = (cnt + (SUBLANES - 1)) // SUBLANES
    local_start = jnp.cumsum(seg, axis=1) - seg
    total = jnp.sum(seg, axis=0)
    padded = (total + BLOCK_GROUPS - 1) // BLOCK_GROUPS * BLOCK_GROUPS
    pend = jnp.cumsum(padded)
    pstart = pend - padded
    global_start = pstart[None, :] + jnp.cumsum(seg, axis=0) - seg
    nused = (pend[-1] // BLOCK_GROUPS).astype(jnp.int32).reshape(1)
    return dict(
        nseg=seg.reshape(-1).astype(jnp.int32),
        local_start=local_start.reshape(-1).astype(jnp.int32),
        global_start=global_start.reshape(-1).astype(jnp.int32),
        tail_start=(pstart + total).astype(jnp.int32),
        tail_len=(padded - total).astype(jnp.int32),
        pend=pend.astype(jnp.int32),
        nused=nused)


def _segment_pieces(nseg, local_start, global_start, tile, make_copy, act):
    for e in range(N_EXPERTS):
        k = tile * N_EXPERTS + e
        ng = nseg[k]
        l0 = local_start[k]
        g0 = global_start[k]
        for b in reversed(range(SEG_BITS)):
            size = 1 << b
            off = lax.shift_left(lax.shift_right_logical(ng, b + 1), b + 1)

            @pl.when(lax.bitwise_and(lax.shift_right_logical(ng, b), 1) == 1)
            def _():
                act(make_copy(l0 + off, g0 + off, size))


def _dispatch_body(ns_ref, ls_ref, gs_ref, ts_ref, tl_ref, nu_ref, q_hbm, x_ref, g_ref, xs_hbm,
                   q_smem, z, lbuf, zbuf, qsem, dsem, zsem):
    i = pl.program_id(0)
    n = pl.num_programs(0)
    slot = lax.rem(i, 2)

    def q_copy(tile, sl):
        return pltpu.make_async_copy(q_hbm.at[tile, 0], q_smem.at[pl.ds(sl * TILE_CHOICES, TILE_CHOICES)],
                                     qsem.at[sl])

    def seg_copy(sl):
        def make(loc, glob, size):
            return pltpu.make_async_copy(
                lbuf.at[pl.ds((sl * LOCAL_GROUPS + loc) * GROUP_SUBLANES, size * GROUP_SUBLANES)],
                xs_hbm.at[pl.ds(glob * GROUP_SUBLANES, size * GROUP_SUBLANES)], dsem.at[sl])
        return make

    @pl.when(i == 0)
    def _():
        q_copy(0, 0).start()
        lbuf[...] = jnp.zeros(lbuf.shape, F32)

    q_copy(i, slot).wait()

    @pl.when(i + 1 < n)
    def _():
        q_copy(i + 1, 1 - slot).start()

    _store_groups(z, _rms(x_ref[...], g_ref[...]))

    qbase = slot * TILE_CHOICES

    def scatter(ii, carry):
        for s in range(SUBLANES):
            t = ii * SUBLANES + s
            row = z[ii, pl.ds(s, LANE_TILES, stride=SUBLANES), :]
            for k in range(TOP_K):
                lbuf[_row_at(q_smem[qbase + k * MOE_TILE + t]), :] = row
        return carry

    lax.fori_loop(0, TILE_GROUPS, scatter, 0)

    @pl.when(i > 0)
    def _():
        _segment_pieces(ns_ref, ls_ref, gs_ref, i - 1, seg_copy(1 - slot), lambda c: c.wait())

    _segment_pieces(ns_ref, ls_ref, gs_ref, i, seg_copy(slot), lambda c: c.start())

    @pl.when(i == n - 1)
    def _():
        _segment_pieces(ns_ref, ls_ref, gs_ref, i, seg_copy(slot), lambda c: c.wait())
        zbuf[...] = jnp.zeros(zbuf.shape, F32)

        def tail_pieces(act):
            for e in range(N_EXPERTS):
                ng = tl_ref[e]
                g0 = ts_ref[e]
                for b in reversed(range(SEG_BITS)):
                    size = 1 << b
                    off = lax.shift_left(lax.shift_right_logical(ng, b + 1), b + 1)

                    @pl.when(lax.bitwise_and(lax.shift_right_logical(ng, b), 1) == 1)
                    def _():
                        act(pltpu.make_async_copy(
                            zbuf.at[pl.ds(0, size * GROUP_SUBLANES)],
                            xs_hbm.at[pl.ds((g0 + off) * GROUP_SUBLANES, size * GROUP_SUBLANES)],
                            zsem.at[0]))

        tail_pieces(lambda c: c.start())
        tail_pieces(lambda c: c.wait())

        def zero_block(blk, carry):
            c = pltpu.make_async_copy(zbuf, xs_hbm.at[pl.ds(blk * BLOCK_ROWS, BLOCK_ROWS)], zsem.at[0])
            c.start()
            c.wait()
            return carry

        lax.fori_loop(nu_ref[0], xs_hbm.shape[0] // BLOCK_ROWS, zero_block, 0)


def _dispatch(tabs, q, x2, g, n_groups):
    t, d = x2.shape
    tm = MOE_TILE
    nt = t // tm
    grid_spec = pltpu.PrefetchScalarGridSpec(
        num_scalar_prefetch=6,
        grid=(nt,),
        in_specs=[pl.BlockSpec(memory_space=pl.ANY),
                  pl.BlockSpec((tm, d), lambda i, *_: (i, 0)),
                  pl.BlockSpec((1, d), lambda i, *_: (0, 0), pipeline_mode=pl.Buffered(1))],
        out_specs=pl.BlockSpec(memory_space=pl.ANY),
        scratch_shapes=[
            pltpu.SMEM((2 * TILE_CHOICES,), jnp.int32),
            pltpu.VMEM((TILE_GROUPS, GROUP_SUBLANES, LANES), F32),
            pltpu.VMEM((2 * LOCAL_GROUPS * GROUP_SUBLANES, LANES), F32),
            pltpu.VMEM((BLOCK_ROWS, LANES), F32),
            pltpu.SemaphoreType.DMA((2,)),
            pltpu.SemaphoreType.DMA((2,)),
            pltpu.SemaphoreType.DMA((1,)),
        ],
    )
    return pl.pallas_call(
        _dispatch_body,
        grid_spec=grid_spec,
        out_shape=jax.ShapeDtypeStruct((n_groups * GROUP_SUBLANES, LANES), F32),
        compiler_params=pltpu.CompilerParams(
            dimension_semantics=("arbitrary",), vmem_limit_bytes=VMEM_LIMIT),
        name="moe_dispatch",
    )(tabs["nseg"], tabs["local_start"], tabs["global_start"], tabs["tail_start"], tabs["tail_len"],
      tabs["nused"], q, x2, g.reshape(1, -1))


def _expert_body(be_ref, nu_ref, xs_ref, w1_ref, w3_ref, w2_ref, o_ref, actbuf):
    i = pl.program_id(0)
    nused = nu_ref[0]

    @pl.when(i < nused)
    def _():
        h = _load_groups(xs_ref).astype(BF16)
        for j in range(D_EXPERT // EXPERT_CHUNK):
            cs = slice(j * EXPERT_CHUNK, (j + 1) * EXPERT_CHUNK)
            a = jnp.dot(h, w1_ref[0, :, cs], preferred_element_type=F32)
            b = jnp.dot(h, w3_ref[0, :, cs], preferred_element_type=F32)
            actbuf[:, cs] = (a * _sigmoid(a) * b).astype(BF16)
        _store_groups(o_ref, jnp.dot(actbuf[...], w2_ref[0], preferred_element_type=F32))

    @pl.when(i >= nused)
    def _():
        o_ref[...] = jnp.zeros(o_ref.shape, F32)


def _expert_ffn(block_e, nused, xs, w1, w3, w2):
    n_groups = xs.shape[0]
    nb = n_groups // BLOCK_GROUPS
    d = D_MODEL
    w13_spec = pl.BlockSpec((1, d, D_EXPERT), lambda i, be, nu: (be[i], 0, 0))
    w2_spec = pl.BlockSpec((1, D_EXPERT, d), lambda i, be, nu: (be[i], 0, 0))
    blk = (BLOCK_GROUPS, GROUP_SUBLANES, LANES)
    grid_spec = pltpu.PrefetchScalarGridSpec(
        num_scalar_prefetch=2,
        grid=(nb,),
        in_specs=[pl.BlockSpec(blk, lambda i, be, nu: (jnp.clip(i, 0, jnp.maximum(nu[0] - 1, 0)), 0, 0)),
                  w13_spec, w13_spec, w2_spec],
        out_specs=pl.BlockSpec(blk, lambda i, be, nu: (i, 0, 0)),
        scratch_shapes=[pltpu.VMEM((MOE_BLOCK, D_EXPERT), BF16)],
    )
    return pl.pallas_call(
        _expert_body,
        grid_spec=grid_spec,
        out_shape=jax.ShapeDtypeStruct(xs.shape, F32),
        compiler_params=pltpu.CompilerParams(
            dimension_semantics=("arbitrary",), vmem_limit_bytes=EXPERT_VMEM_LIMIT),
        name="moe_experts",
    )(block_e, nused, xs, w1.astype(BF16), w3.astype(BF16), w2.astype(BF16))


def _combine_body(ns_ref, ls_ref, gs_ref, q_hbm, w_hbm, y_hbm, x_ref, g_ref, o_ref,
                  q_smem, w_smem, ybuf, zo, qsem, wsem, ysem):
    i = pl.program_id(0)
    n = pl.num_programs(0)
    slot = lax.rem(i, 2)

    def meta_copies(tile, sl):
        dst = pl.ds(sl * TILE_CHOICES, TILE_CHOICES)
        return (pltpu.make_async_copy(q_hbm.at[tile, 0], q_smem.at[dst], qsem.at[sl]),
                pltpu.make_async_copy(w_hbm.at[tile, 0], w_smem.at[dst], wsem.at[sl]))

    def seg_copy(sl):
        def make(loc, glob, size):
            return pltpu.make_async_copy(
                y_hbm.at[pl.ds(glob * GROUP_SUBLANES, size * GROUP_SUBLANES)],
                ybuf.at[pl.ds((sl * LOCAL_GROUPS + loc) * GROUP_SUBLANES, size * GROUP_SUBLANES)],
                ysem.at[sl])
        return make

    def fetch(tile, sl):
        for c in meta_copies(tile, sl):
            c.start()
        _segment_pieces(ns_ref, ls_ref, gs_ref, tile, seg_copy(sl), lambda c: c.start())

    @pl.when(i == 0)
    def _():
        fetch(0, 0)

    @pl.when(i + 1 < n)
    def _():
        fetch(i + 1, 1 - slot)

    for c in meta_copies(i, slot):
        c.wait()
    _segment_pieces(ns_ref, ls_ref, gs_ref, i, seg_copy(slot), lambda c: c.wait())

    qbase = slot * TILE_CHOICES

    def gather(ii, carry):
        for s in range(SUBLANES):
            t0 = qbase + ii * SUBLANES + s
            t1 = t0 + MOE_TILE
            zo[ii, pl.ds(s, LANE_TILES, stride=SUBLANES), :] = (
                w_smem[t0] * ybuf[_row_at(q_smem[t0]), :] + w_smem[t1] * ybuf[_row_at(q_smem[t1]), :])
        return carry

    lax.fori_loop(0, TILE_GROUPS, gather, 0)
    o_ref[...] = _rms(x_ref[...] + _load_groups(zo), g_ref[...])


def _combine(tabs, q, wts, y, x2, g):
    t, d = x2.shape
    tm = MOE_TILE
    grid_spec = pltpu.PrefetchScalarGridSpec(
        num_scalar_prefetch=3,
        grid=(t // tm,),
        in_specs=[pl.BlockSpec(memory_space=pl.ANY),
                  pl.BlockSpec(memory_space=pl.ANY),
                  pl.BlockSpec(memory_space=pl.ANY),
                  pl.BlockSpec((tm, d), lambda i, *_: (i, 0)),
                  pl.BlockSpec((1, d), lambda i, *_: (0, 0), pipeline_mode=pl.Buffered(1))],
        out_specs=pl.BlockSpec((tm, d), lambda i, *_: (i, 0)),
        scratch_shapes=[
            pltpu.SMEM((2 * TILE_CHOICES,), jnp.int32),
            pltpu.SMEM((2 * TILE_CHOICES,), F32),
            pltpu.VMEM((2 * LOCAL_GROUPS * GROUP_SUBLANES, LANES), F32),
            pltpu.VMEM((TILE_GROUPS, GROUP_SUBLANES, LANES), F32),
            pltpu.SemaphoreType.DMA((2,)),
            pltpu.SemaphoreType.DMA((2,)),
            pltpu.SemaphoreType.DMA((2,)),
        ],
    )
    return pl.pallas_call(
        _combine_body,
        grid_spec=grid_spec,
        out_shape=jax.ShapeDtypeStruct(x2.shape, F32),
        compiler_params=pltpu.CompilerParams(
            dimension_semantics=("arbitrary",), vmem_limit_bytes=VMEM_LIMIT),
        name="moe_combine_norm",
    )(tabs["nseg"], tabs["local_start"], tabs["global_start"], q, wts, y, x2, g.reshape(1, -1))


def _moe_and_final_norm(x2, ffn_g, w_router, w1, w3, w2, final_g):
    t, d = x2.shape
    nt = t // MOE_TILE
    max_rows = TOP_K * t + nt * N_EXPERTS * (SUBLANES - 1) + N_EXPERTS * (MOE_BLOCK - 1)
    n_blocks = -(-max_rows // MOE_BLOCK)
    q, wts, cnt3 = _router(x2, ffn_g, w_router)
    tabs = _segment_tables(cnt3[:, :, 0])
    block_start = jnp.arange(n_blocks, dtype=jnp.int32) * BLOCK_GROUPS
    block_e = jnp.minimum(jnp.sum(block_start[:, None] >= tabs["pend"][None, :], axis=1),
                          N_EXPERTS - 1).astype(jnp.int32)
    xs = _dispatch(tabs, q, x2, ffn_g, n_blocks * BLOCK_GROUPS)
    y = _expert_ffn(block_e, tabs["nused"], xs.reshape(-1, GROUP_SUBLANES, LANES), w1, w3, w2)
    return _combine(tabs, q, wts, y.reshape(-1, LANES), x2, final_g)


def kernel(x, mix_norm_g, w_in, conv_w, conv_b, conv_ln_g, conv_ln_b, conv_w_out, pool_w, pool_scale,
           sgu_ln_g, sgu_ln_b, sgu_w_s, sgu_b_s, sgu_w_out, w_out, ffn_norm_g, dense_w1, dense_w3,
           dense_w2, router_w, expert_w1, expert_w3, expert_w2, final_norm_g):
    b, s, d = x.shape

    def mixer(xx, l):
        return _mixer_layer(xx, mix_norm_g[l], w_in[l], conv_w[l], conv_b[l], conv_ln_g[l], conv_ln_b[l],
                            conv_w_out[l], pool_w[l], pool_scale[l], sgu_ln_g[l], sgu_ln_b[l],
                            sgu_w_s[l], sgu_b_s[l], sgu_w_out[l], w_out[l])

    x = mixer(x, 0)
    x2 = _dense_ffn(x.reshape(b * s, d), ffn_norm_g[0], dense_w1[0], dense_w3[0], dense_w2[0])
    x = mixer(x2.reshape(b, s, d), 1)
    out = _moe_and_final_norm(x.reshape(b * s, d), ffn_norm_g[1], router_w[0], expert_w1[0],
                              expert_w3[0], expert_w2[0], final_norm_g)
    return out.reshape(b, s, d)
```

```python
import jax
import jax.numpy as jnp
from jax import lax
from jax.experimental import pallas as pl
from jax.experimental.pallas import tpu as pltpu

F32 = jnp.float32
BF16 = jnp.bfloat16

D_MODEL = 1024
DEPTH = 2
D_CONV = 512
CONV_WIDTH = 31
D_POOL = 512
POOL_WINDOWS = (2, 4, 8, 16)
N_POOL_GROUPS = 4
POOL_GROUP = D_POOL // N_POOL_GROUPS
POOL_OUT_GROUP = D_MODEL // N_POOL_GROUPS
D_SGU = 512
SGU_HEADS = 4
SGU_HEAD_DIM = D_SGU // SGU_HEADS
CHUNK = 128
SPLIT_A = 2 * D_CONV
SPLIT_B = SPLIT_A + D_POOL
SPLIT_U = SPLIT_B + D_SGU
SPLIT_V = SPLIT_U + D_SGU
GATE_A = SPLIT_V
GATE_B = GATE_A + D_MODEL
GATE_C = GATE_B + D_MODEL
D_IN = GATE_C + D_MODEL
D_FF = 2816
N_EXPERTS = 8
TOP_K = 2
D_EXPERT = 3584
MOE_BLOCK = 512
EPS = 1e-6

SUBLANES = 8
SUBLANE_SHIFT = 3
LANES = 128
LANE_TILES = D_MODEL // LANES
HALO = 32
SEQ_TILE = 512
CONV_ROWS = 32
PROJ_CHUNK = 256
FFN_TILE = 512
MOE_TILE = 1024
CUM_CHUNK = 256
EXPERT_CHUNK = 512
VMEM_LIMIT = 56 * 1024 * 1024
EXPERT_VMEM_LIMIT = 62 * 1024 * 1024

GROUP_SUBLANES = LANE_TILES * SUBLANES
TILE_GROUPS = MOE_TILE // SUBLANES
BLOCK_GROUPS = MOE_BLOCK // SUBLANES
LOCAL_GROUPS = TOP_K * TILE_GROUPS + N_EXPERTS
SEG_BITS = TILE_GROUPS.bit_length()
TILE_CHOICES = TOP_K * MOE_TILE
BLOCK_ROWS = BLOCK_GROUPS * GROUP_SUBLANES


def _rms(x, g):
    ms = jnp.mean(x * x, axis=-1, keepdims=True)
    return x * lax.rsqrt(ms + EPS) * g


def _layernorm(x, g, b):
    mu = jnp.mean(x, axis=-1, keepdims=True)
    xc = x - mu
    var = jnp.mean(xc * xc, axis=-1, keepdims=True)
    return xc * lax.rsqrt(var + EPS) * g + b


def _sigmoid(x):
    return 1.0 / (1.0 + jnp.exp(-x))


def _const_spec(shape):
    return pl.BlockSpec(shape, lambda *_: (0,) * len(shape), pipeline_mode=pl.Buffered(1))


def _to_groups(v):
    rows = v.shape[0]
    return [v[:, j * LANES:(j + 1) * LANES].reshape(rows // SUBLANES, SUBLANES, LANES)
            for j in range(LANE_TILES)]


def _store_groups(ref, v):
    for j, piece in enumerate(_to_groups(v)):
        ref[:, j * SUBLANES:(j + 1) * SUBLANES, :] = piece


def _load_groups(ref):
    rows = ref.shape[0] * SUBLANES
    return jnp.concatenate(
        [ref[:, j * SUBLANES:(j + 1) * SUBLANES, :].reshape(rows, LANES) for j in range(LANE_TILES)],
        axis=-1)


def _row_at(offset):
    return pl.ds(offset, LANE_TILES, stride=SUBLANES)


def _mixer_body(x_ref, ng_ref, win_ref, cw_ref, cb_ref, clg_ref, clb_ref, cwo_ref,
                pw_ref, ps_ref, slg_ref, slb_ref, sws_ref, sbs_ref, swo_ref, wo_ref,
                o_ref, abuf, cbuf, pbuf, tbuf, usbuf, uvbuf, gbuf):
    ts = SEQ_TILE
    s = pl.program_id(1)

    @pl.when(s == 0)
    def _():
        abuf[:, 0:HALO, :] = jnp.zeros((D_CONV // LANES, HALO, LANES), F32)
        pbuf[0:HALO, :] = jnp.zeros((HALO, D_POOL), F32)

    x = x_ref[0]
    h = _rms(x, ng_ref[...]).astype(BF16)

    def proj(lo, hi):
        return jnp.dot(h, win_ref[:, lo:hi], preferred_element_type=F32)

    def project_puv(c):
        lo = SPLIT_A + c * PROJ_CHUNK
        if lo < SPLIT_B:
            pbuf[HALO:HALO + ts, lo - SPLIT_A:lo - SPLIT_A + PROJ_CHUNK] = proj(lo, lo + PROJ_CHUNK)
        else:
            uvbuf[:, lo - SPLIT_B:lo - SPLIT_B + PROJ_CHUNK] = proj(lo, lo + PROJ_CHUNK)

    def project_gate(c):
        lo = c * PROJ_CHUNK
        gbuf[:, lo:lo + PROJ_CHUNK] = _sigmoid(proj(GATE_A + lo, GATE_A + lo + PROJ_CHUNK))

    n_puv = (SPLIT_V - SPLIT_A) // PROJ_CHUNK
    n_gate = (D_IN - GATE_A) // PROJ_CHUNK

    ain = proj(0, SPLIT_A)
    a = ain[:, :D_CONV] * _sigmoid(ain[:, D_CONV:])
    n_slab = D_CONV // LANES
    for lt in range(n_slab):
        abuf[lt, HALO:HALO + ts, :] = a[:, lt * LANES:(lt + 1) * LANES]

    pieces = ([(project_puv, c) for c in range(n_puv)] + [(project_gate, c) for c in range(n_gate)])
    chunks = [(lt, r0) for lt in range(n_slab) for r0 in range(0, ts, CONV_ROWS)]
    groups = CONV_ROWS // SUBLANES
    first_tap = HALO - (CONV_WIDTH - 1)
    for ci, (lt, r0) in enumerate(chunks):
        ls = slice(lt * LANES, (lt + 1) * LANES)
        acc = jnp.broadcast_to(cb_ref[:, ls][None], (groups, SUBLANES, LANES))
        for k in range(CONV_WIDTH):
            win = abuf[lt, r0 + first_tap + k:r0 + first_tap + k + CONV_ROWS, :]
            acc = acc + win.reshape(groups, SUBLANES, LANES) * cw_ref[k, :, ls][None]
        cbuf[r0:r0 + CONV_ROWS, ls] = acc.reshape(CONV_ROWS, LANES)
        for pi, (project, c) in enumerate(pieces):
            if (pi * len(chunks)) // len(pieces) == ci:
                project(c)
    abuf[:, 0:HALO, :] = abuf[:, ts:ts + HALO, :]

    ca = _layernorm(cbuf[...], clg_ref[...], clb_ref[...])
    ca = (ca * _sigmoid(ca)).astype(BF16)
    y_a = jnp.dot(ca, cwo_ref[...], preferred_element_type=F32)
    mixed = gbuf[:, 0:D_MODEL] * y_a

    tbuf[0:SUBLANES, :] = jnp.zeros((SUBLANES, POOL_GROUP), F32)
    pos1 = (s * ts + 1 + lax.broadcasted_iota(jnp.int32, (ts, 1), 0))
    ext = ts + HALO
    yb_parts = []
    for g, w in enumerate(POOL_WINDOWS):
        lo = g * POOL_GROUP
        cur = pbuf[:, lo:lo + POOL_GROUP]
        sh = 1
        while sh < w:
            tbuf[SUBLANES:SUBLANES + ext, :] = cur
            cur = cur + tbuf[SUBLANES - sh:SUBLANES - sh + ext, :]
            sh *= 2
        cnt = jnp.minimum(pos1, w).astype(F32)
        pooled = cur[HALO:, :] / cnt - pbuf[HALO:HALO + ts, lo:lo + POOL_GROUP]
        yb_parts.append(jnp.dot(pooled.astype(BF16), pw_ref[g], preferred_element_type=F32))
    pbuf[0:HALO, :] = pbuf[ts:ts + HALO, :]
    y_b = jnp.concatenate(yb_parts, axis=-1) * ps_ref[...]
    mixed = mixed + gbuf[:, D_MODEL:2 * D_MODEL] * y_b

    vn = _layernorm(uvbuf[:, D_SGU:2 * D_SGU], slg_ref[...], slb_ref[...]).astype(BF16)
    row = lax.broadcasted_iota(jnp.int32, (CHUNK, CHUNK), 0)
    col = lax.broadcasted_iota(jnp.int32, (CHUNK, CHUNK), 1)
    for hh in range(SGU_HEADS):
        wm = jnp.where(col <= row, sws_ref[hh], 0.0).astype(BF16)
        c0 = hh * SGU_HEAD_DIM
        for c in range(ts // CHUNK):
            r0 = c * CHUNK
            sp = jnp.dot(wm, vn[r0:r0 + CHUNK, c0:c0 + SGU_HEAD_DIM], preferred_element_type=F32)
            sp = sp + sbs_ref[hh]
            usbuf[r0:r0 + CHUNK, c0:c0 + SGU_HEAD_DIM] = (
                uvbuf[r0:r0 + CHUNK, c0:c0 + SGU_HEAD_DIM] * sp).astype(BF16)
    y_c = jnp.dot(usbuf[...], swo_ref[...], preferred_element_type=F32)
    mixed = mixed + gbuf[:, 2 * D_MODEL:3 * D_MODEL] * y_c

    o_ref[0] = x + jnp.dot(mixed.astype(BF16), wo_ref[...], preferred_element_type=F32)


def _mixer_layer(x, ng, w_in, cw, cb, clg, clb, cwo, pw, ps, slg, slb, sws, sbs, swo, wo):
    b, s, d = x.shape
    ts = SEQ_TILE
    row = lambda v: v.reshape(1, -1)
    sbs_b = jnp.broadcast_to(sbs[:, :, None], (SGU_HEADS, CHUNK, SGU_HEAD_DIM))
    cw_b = jnp.broadcast_to(cw[:, None, :], (CONV_WIDTH, SUBLANES, D_CONV))
    cb_b = jnp.broadcast_to(cb[None, :], (SUBLANES, D_CONV))
    args = (x, row(ng), w_in.astype(BF16), cw_b, cb_b, row(clg), row(clb), cwo.astype(BF16),
            pw.astype(BF16), row(ps), row(slg), row(slb), sws, sbs_b, swo.astype(BF16), wo.astype(BF16))
    x_spec = pl.BlockSpec((1, ts, d), lambda bi, si: (bi, si, 0))
    in_specs = [x_spec] + [_const_spec(a.shape) for a in args[1:]]
    return pl.pallas_call(
        _mixer_body,
        grid=(b, s // ts),
        in_specs=in_specs,
        out_specs=x_spec,
        out_shape=jax.ShapeDtypeStruct(x.shape, F32),
        scratch_shapes=[
            pltpu.VMEM((D_CONV // LANES, ts + HALO, LANES), F32),
            pltpu.VMEM((ts, D_CONV), F32),
            pltpu.VMEM((ts + HALO, D_POOL), F32),
            pltpu.VMEM((SUBLANES + ts + HALO, POOL_GROUP), F32),
            pltpu.VMEM((ts, D_SGU), BF16),
            pltpu.VMEM((ts, 2 * D_SGU), F32),
            pltpu.VMEM((ts, 3 * D_MODEL), F32),
        ],
        compiler_params=pltpu.CompilerParams(
            dimension_semantics=("arbitrary", "arbitrary"), vmem_limit_bytes=VMEM_LIMIT),
        name="token_mixer",
    )(*args)


def _dense_ffn_body(x_ref, g_ref, w1_ref, w3_ref, w2_ref, o_ref):
    x = x_ref[...]
    h = _rms(x, g_ref[...]).astype(BF16)
    a = jnp.dot(h, w1_ref[...], preferred_element_type=F32)
    b = jnp.dot(h, w3_ref[...], preferred_element_type=F32)
    act = (a * _sigmoid(a) * b).astype(BF16)
    o_ref[...] = x + jnp.dot(act, w2_ref[...], preferred_element_type=F32)


def _dense_ffn(x2, g, w1, w3, w2):
    t, d = x2.shape
    tm = FFN_TILE
    args = (x2, g.reshape(1, -1), w1.astype(BF16), w3.astype(BF16), w2.astype(BF16))
    x_spec = pl.BlockSpec((tm, d), lambda i: (i, 0))
    return pl.pallas_call(
        _dense_ffn_body,
        grid=(t // tm,),
        in_specs=[x_spec] + [_const_spec(a.shape) for a in args[1:]],
        out_specs=x_spec,
        out_shape=jax.ShapeDtypeStruct(x2.shape, F32),
        compiler_params=pltpu.CompilerParams(
            dimension_semantics=("arbitrary",), vmem_limit_bytes=VMEM_LIMIT),
        name="dense_ffn",
    )(*args)


def _router_body(x_ref, g_ref, wr_ref, tri_ref, q_ref, wt_ref, cnt_ref):
    tm = MOE_TILE
    h = _rms(x_ref[...], g_ref[...])
    w = wr_ref[...]
    h_hi = h.astype(BF16)
    h_lo = (h - h_hi.astype(F32)).astype(BF16)
    w_hi = w.astype(BF16)
    w_lo = (w - w_hi.astype(F32)).astype(BF16)

    def dot_nt(a, b):
        return lax.dot_general(a, b, (((1,), (1,)), ((), ())), preferred_element_type=F32)

    logits = dot_nt(w_hi, h_hi) + (dot_nt(w_hi, h_lo) + dot_nt(w_lo, h_hi))
    eid = lax.broadcasted_iota(jnp.int32, logits.shape, 0)
    m1 = jnp.max(logits, axis=0, keepdims=True)
    i1 = jnp.min(jnp.where(logits == m1, eid, N_EXPERTS), axis=0, keepdims=True)
    rest = jnp.where(eid == i1, -jnp.inf, logits)
    m2 = jnp.max(rest, axis=0, keepdims=True)
    i2 = jnp.min(jnp.where(rest == m2, eid, N_EXPERTS), axis=0, keepdims=True)
    e2 = jnp.exp(m2 - m1)
    den = 1.0 + e2
    wt_ref[0] = jnp.concatenate([1.0 / den, e2 / den], axis=1)

    oh1 = eid == i1
    oh2 = eid == i2
    onehot = jnp.where(oh1 | oh2, 1.0, 0.0)
    oh_bf = onehot.astype(BF16)
    parts = []
    carry = jnp.zeros((N_EXPERTS, 1), F32)
    for c in range(tm // CUM_CHUNK):
        cs = slice(c * CUM_CHUNK, (c + 1) * CUM_CHUNK)
        parts.append(jnp.dot(oh_bf[:, cs], tri_ref[...], preferred_element_type=F32) + carry)
        carry = carry + jnp.sum(onehot[:, cs], axis=1, keepdims=True)
    incl = jnp.concatenate(parts, axis=1)
    cnt = carry.astype(jnp.int32)
    seg = (lax.shift_right_logical(cnt + (SUBLANES - 1), SUBLANE_SHIFT) * SUBLANES).astype(F32)
    erow = lax.broadcasted_iota(jnp.int32, (N_EXPERTS, 1), 0)
    seg_start = jnp.zeros((N_EXPERTS, 1), F32)
    for e in range(N_EXPERTS - 1):
        seg_start = seg_start + jnp.where(erow > e, seg[e:e + 1, :], 0.0)
    base = seg_start + (incl - onehot)
    q1 = jnp.sum(jnp.where(oh1, base, 0.0), axis=0, keepdims=True)
    q2 = jnp.sum(jnp.where(oh2, base, 0.0), axis=0, keepdims=True)
    q = jnp.concatenate([q1, q2], axis=1).astype(jnp.int32)
    half = lax.rem(pl.program_id(0), 2) * (LOCAL_GROUPS * GROUP_SUBLANES)
    q_ref[0] = (lax.shift_right_logical(q, SUBLANE_SHIFT) * GROUP_SUBLANES + lax.bitwise_and(q, SUBLANES - 1)
                + half)
    cnt_ref[0] = jnp.broadcast_to(cnt, (N_EXPERTS, LANES))


def _router(x2, g, w_router):
    t, d = x2.shape
    tm = MOE_TILE
    nt = t // tm
    tri = jnp.triu(jnp.ones((CUM_CHUNK, CUM_CHUNK), BF16))
    tile_spec = pl.BlockSpec((1, 1, TOP_K * tm), lambda i: (i, 0, 0))
    return pl.pallas_call(
        _router_body,
        grid=(nt,),
        in_specs=[pl.BlockSpec((tm, d), lambda i: (i, 0)),
                  _const_spec((1, d)), _const_spec((N_EXPERTS, d)), _const_spec((CUM_CHUNK, CUM_CHUNK))],
        out_specs=[tile_spec, tile_spec, pl.BlockSpec((1, N_EXPERTS, LANES), lambda i: (i, 0, 0))],
        out_shape=[jax.ShapeDtypeStruct((nt, 1, TOP_K * tm), jnp.int32),
                   jax.ShapeDtypeStruct((nt, 1, TOP_K * tm), F32),
                   jax.ShapeDtypeStruct((nt, N_EXPERTS, LANES), jnp.int32)],
        compiler_params=pltpu.CompilerParams(
            dimension_semantics=("arbitrary",), vmem_limit_bytes=VMEM_LIMIT),
        name="moe_router",
    )(x2, g.reshape(1, -1), w_router.T, tri)


def _segment_tables(cnt):
    seg = (cnt + (SUBLANES - 1)) // SUBLANES
    local_start = jnp.cumsum(seg, axis=1) - seg
    total = jnp.sum(seg, axis=0)
    padded = (total + BLOCK_GROUPS - 1) // BLOCK_GROUPS * BLOCK_GROUPS
    pend = jnp.cumsum(padded)
    pstart = pend - padded
    global_start = pstart[None, :] + jnp.cumsum(seg, axis=0) - seg
    nused = (pend[-1] // BLOCK_GROUPS).astype(jnp.int32).reshape(1)
    return dict(
        nseg=seg.reshape(-1).astype(jnp.int32),
        local_start=local_start.reshape(-1).astype(jnp.int32),
        global_start=global_start.reshape(-1).astype(jnp.int32),
        tail_start=(pstart + total).astype(jnp.int32),
        tail_len=(padded - total).astype(jnp.int32),
        pend=pend.astype(jnp.int32),
        nused=nused)


def _segment_pieces(nseg, local_start, global_start, tile, make_copy, act):
    for e in range(N_EXPERTS):
        k = tile * N_EXPERTS + e
        ng = nseg[k]
        l0 = local_start[k]
        g0 = global_start[k]
        for b in reversed(range(SEG_BITS)):
            size = 1 << b
            off = lax.shift_left(lax.shift_right_logical(ng, b + 1), b + 1)

            @pl.when(lax.bitwise_and(lax.shift_right_logical(ng, b), 1) == 1)
            def _():
                act(make_copy(l0 + off, g0 + off, size))


def _dispatch_body(ns_ref, ls_ref, gs_ref, ts_ref, tl_ref, nu_ref, q_hbm, x_ref, g_ref, xs_hbm,
                   q_smem, z, lbuf, zbuf, qsem, dsem, zsem):
    i = pl.program_id(0)
    n = pl.num_programs(0)
    slot = lax.rem(i, 2)

    def q_copy(tile, sl):
        return pltpu.make_async_copy(q_hbm.at[tile, 0], q_smem.at[pl.ds(sl * TILE_CHOICES, TILE_CHOICES)],
                                     qsem.at[sl])

    def seg_copy(sl):
        def make(loc, glob, size):
            return pltpu.make_async_copy(
                lbuf.at[pl.ds((sl * LOCAL_GROUPS + loc) * GROUP_SUBLANES, size * GROUP_SUBLANES)],
                xs_hbm.at[pl.ds(glob * GROUP_SUBLANES, size * GROUP_SUBLANES)], dsem.at[sl])
        return make

    @pl.when(i == 0)
    def _():
        q_copy(0, 0).start()
        lbuf[...] = jnp.zeros(lbuf.shape, F32)

    q_copy(i, slot).wait()

    @pl.when(i + 1 < n)
    def _():
        q_copy(i + 1, 1 - slot).start()

    _store_groups(z, _rms(x_ref[...], g_ref[...]))

    qbase = slot * TILE_CHOICES

    def scatter(ii, carry):
        for s in range(SUBLANES):
            t = ii * SUBLANES + s
            row = z[ii, pl.ds(s, LANE_TILES, stride=SUBLANES), :]
            for k in range(TOP_K):
                lbuf[_row_at(q_smem[qbase + k * MOE_TILE + t]), :] = row
        return carry

    lax.fori_loop(0, TILE_GROUPS, scatter, 0)

    @pl.when(i > 0)
    def _():
        _segment_pieces(ns_ref, ls_ref, gs_ref, i - 1, seg_copy(1 - slot), lambda c: c.wait())

    _segment_pieces(ns_ref, ls_ref, gs_ref, i, seg_copy(slot), lambda c: c.start())

    @pl.when(i == n - 1)
    def _():
        _segment_pieces(ns_ref, ls_ref, gs_ref, i, seg_copy(slot), lambda c: c.wait())
        zbuf[...] = jnp.zeros(zbuf.shape, F32)

        def tail_pieces(act):
            for e in range(N_EXPERTS):
                ng = tl_ref[e]
                g0 = ts_ref[e]
                for b in reversed(range(SEG_BITS)):
                    size = 1 << b
                    off = lax.shift_left(lax.shift_right_logical(ng, b + 1), b + 1)

                    @pl.when(lax.bitwise_and(lax.shift_right_logical(ng, b), 1) == 1)
                    def _():
                        act(pltpu.make_async_copy(
                            zbuf.at[pl.ds(0, size * GROUP_SUBLANES)],
                            xs_hbm.at[pl.ds((g0 + off) * GROUP_SUBLANES, size * GROUP_SUBLANES)],
                            zsem.at[0]))

        tail_pieces(lambda c: c.start())
        tail_pieces(lambda c: c.wait())

        def zero_block(blk, carry):
            c = pltpu.make_async_copy(zbuf, xs_hbm.at[pl.ds(blk * BLOCK_ROWS, BLOCK_ROWS)], zsem.at[0])
            c.start()
            c.wait()
            return carry

        lax.fori_loop(nu_ref[0], xs_hbm.shape[0] // BLOCK_ROWS, zero_block, 0)


def _dispatch(tabs, q, x2, g, n_groups):
    t, d = x2.shape
    tm = MOE_TILE
    nt = t // tm
    grid_spec = pltpu.PrefetchScalarGridSpec(
        num_scalar_prefetch=6,
        grid=(nt,),
        in_specs=[pl.BlockSpec(memory_space=pl.ANY),
                  pl.BlockSpec((tm, d), lambda i, *_: (i, 0)),
                  pl.BlockSpec((1, d), lambda i, *_: (0, 0), pipeline_mode=pl.Buffered(1))],
        out_specs=pl.BlockSpec(memory_space=pl.ANY),
        scratch_shapes=[
            pltpu.SMEM((2 * TILE_CHOICES,), jnp.int32),
            pltpu.VMEM((TILE_GROUPS, GROUP_SUBLANES, LANES), F32),
            pltpu.VMEM((2 * LOCAL_GROUPS * GROUP_SUBLANES, LANES), F32),
            pltpu.VMEM((BLOCK_ROWS, LANES), F32),
            pltpu.SemaphoreType.DMA((2,)),
            pltpu.SemaphoreType.DMA((2,)),
            pltpu.SemaphoreType.DMA((1,)),
        ],
    )
    return pl.pallas_call(
        _dispatch_body,
        grid_spec=grid_spec,
        out_shape=jax.ShapeDtypeStruct((n_groups * GROUP_SUBLANES, LANES), F32),
        compiler_params=pltpu.CompilerParams(
            dimension_semantics=("arbitrary",), vmem_limit_bytes=VMEM_LIMIT),
        name="moe_dispatch",
    )(tabs["nseg"], tabs["local_start"], tabs["global_start"], tabs["tail_start"], tabs["tail_len"],
      tabs["nused"], q, x2, g.reshape(1, -1))


def _expert_body(be_ref, nu_ref, xs_ref, w1_ref, w3_ref, w2_ref, o_ref, actbuf):
    i = pl.program_id(0)
    nused = nu_ref[0]

    @pl.when(i < nused)
    def _():
        h = _load_groups(xs_ref).astype(BF16)
        for j in range(D_EXPERT // EXPERT_CHUNK):
            cs = slice(j * EXPERT_CHUNK, (j + 1) * EXPERT_CHUNK)
            a = jnp.dot(h, w1_ref[0, :, cs], preferred_element_type=F32)
            b = jnp.dot(h, w3_ref[0, :, cs], preferred_element_type=F32)
            actbuf[:, cs] = (a * _sigmoid(a) * b).astype(BF16)
        _store_groups(o_ref, jnp.dot(actbuf[...], w2_ref[0], preferred_element_type=F32))

    @pl.when(i >= nused)
    def _():
        o_ref[...] = jnp.zeros(o_ref.shape, F32)


def _expert_ffn(block_e, nused, xs, w1, w3, w2):
    n_groups = xs.shape[0]
    nb = n_groups // BLOCK_GROUPS
    d = D_MODEL
    w13_spec = pl.BlockSpec((1, d, D_EXPERT), lambda i, be, nu: (be[i], 0, 0))
    w2_spec = pl.BlockSpec((1, D_EXPERT, d), lambda i, be, nu: (be[i], 0, 0))
    blk = (BLOCK_GROUPS, GROUP_SUBLANES, LANES)
    grid_spec = pltpu.PrefetchScalarGridSpec(
        num_scalar_prefetch=2,
        grid=(nb,),
        in_specs=[pl.BlockSpec(blk, lambda i, be, nu: (jnp.clip(i, 0, jnp.maximum(nu[0] - 1, 0)), 0, 0)),
                  w13_spec, w13_spec, w2_spec],
        out_specs=pl.BlockSpec(blk, lambda i, be, nu: (i, 0, 0)),
        scratch_shapes=[pltpu.VMEM((MOE_BLOCK, D_EXPERT), BF16)],
    )
    return pl.pallas_call(
        _expert_body,
        grid_spec=grid_spec,
        out_shape=jax.ShapeDtypeStruct(xs.shape, F32),
        compiler_params=pltpu.CompilerParams(
            dimension_semantics=("arbitrary",), vmem_limit_bytes=EXPERT_VMEM_LIMIT),
        name="moe_experts",
    )(block_e, nused, xs, w1.astype(BF16), w3.astype(BF16), w2.astype(BF16))


def _combine_body(ns_ref, ls_ref, gs_ref, q_hbm, w_hbm, y_hbm, x_ref, g_ref, o_ref,
                  q_smem, w_smem, ybuf, zo, qsem, wsem, ysem):
    i = pl.program_id(0)
    n = pl.num_programs(0)
    slot = lax.rem(i, 2)

    def meta_copies(tile, sl):
        dst = pl.ds(sl * TILE_CHOICES, TILE_CHOICES)
        return (pltpu.make_async_copy(q_hbm.at[tile, 0], q_smem.at[dst], qsem.at[sl]),
                pltpu.make_async_copy(w_hbm.at[tile, 0], w_smem.at[dst], wsem.at[sl]))

    def seg_copy(sl):
        def make(loc, glob, size):
            return pltpu.make_async_copy(
                y_hbm.at[pl.ds(glob * GROUP_SUBLANES, size * GROUP_SUBLANES)],
                ybuf.at[pl.ds((sl * LOCAL_GROUPS + loc) * GROUP_SUBLANES, size * GROUP_SUBLANES)],
                ysem.at[sl])
        return make

    def fetch(tile, sl):
        for c in meta_copies(tile, sl):
            c.start()
        _segment_pieces(ns_ref, ls_ref, gs_ref, tile, seg_copy(sl), lambda c: c.start())

    @pl.when(i == 0)
    def _():
        fetch(0, 0)

    @pl.when(i + 1 < n)
    def _():
        fetch(i + 1, 1 - slot)

    for c in meta_copies(i, slot):
        c.wait()
    _segment_pieces(ns_ref, ls_ref, gs_ref, i, seg_copy(slot), lambda c: c.wait())

    qbase = slot * TILE_CHOICES

    def gather(ii, carry):
        for s in range(SUBLANES):
            t0 = qbase + ii * SUBLANES + s
            t1 = t0 + MOE_TILE
            zo[ii, pl.ds(s, LANE_TILES, stride=SUBLANES), :] = (
                w_smem[t0] * ybuf[_row_at(q_smem[t0]), :] + w_smem[t1] * ybuf[_row_at(q_smem[t1]), :])
        return carry

    lax.fori_loop(0, TILE_GROUPS, gather, 0)
    o_ref[...] = _rms(x_ref[...] + _load_groups(zo), g_ref[...])


def _combine(tabs, q, wts, y, x2, g):
    t, d = x2.shape
    tm = MOE_TILE
    grid_spec = pltpu.PrefetchScalarGridSpec(
        num_scalar_prefetch=3,
        grid=(t // tm,),
        in_specs=[pl.BlockSpec(memory_space=pl.ANY),
                  pl.BlockSpec(memory_space=pl.ANY),
                  pl.BlockSpec(memory_space=pl.ANY),
                  pl.BlockSpec((tm, d), lambda i, *_: (i, 0)),
                  pl.BlockSpec((1, d), lambda i, *_: (0, 0), pipeline_mode=pl.Buffered(1))],
        out_specs=pl.BlockSpec((tm, d), lambda i, *_: (i, 0)),
        scratch_shapes=[
            pltpu.SMEM((2 * TILE_CHOICES,), jnp.int32),
            pltpu.SMEM((2 * TILE_CHOICES,), F32),
            pltpu.VMEM((2 * LOCAL_GROUPS * GROUP_SUBLANES, LANES), F32),
            pltpu.VMEM((TILE_GROUPS, GROUP_SUBLANES, LANES), F32),
            pltpu.SemaphoreType.DMA((2,)),
            pltpu.SemaphoreType.DMA((2,)),
            pltpu.SemaphoreType.DMA((2,)),
        ],
    )
    return pl.pallas_call(
        _combine_body,
        grid_spec=grid_spec,
        out_shape=jax.ShapeDtypeStruct(x2.shape, F32),
        compiler_params=pltpu.CompilerParams(
            dimension_semantics=("arbitrary",), vmem_limit_bytes=VMEM_LIMIT),
        name="moe_combine_norm",
    )(tabs["nseg"], tabs["local_start"], tabs["global_start"], q, wts, y, x2, g.reshape(1, -1))


def _moe_and_final_norm(x2, ffn_g, w_router, w1, w3, w2, final_g):
    t, d = x2.shape
    nt = t // MOE_TILE
    max_rows = TOP_K * t + nt * N_EXPERTS * (SUBLANES - 1) + N_EXPERTS * (MOE_BLOCK - 1)
    n_blocks = -(-max_rows // MOE_BLOCK)
    q, wts, cnt3 = _router(x2, ffn_g, w_router)
    tabs = _segment_tables(cnt3[:, :, 0])
    block_start = jnp.arange(n_blocks, dtype=jnp.int32) * BLOCK_GROUPS
    block_e = jnp.minimum(jnp.sum(block_start[:, None] >= tabs["pend"][None, :], axis=1),
                          N_EXPERTS - 1).astype(jnp.int32)
    xs = _dispatch(tabs, q, x2, ffn_g, n_blocks * BLOCK_GROUPS)
    y = _expert_ffn(block_e, tabs["nused"], xs.reshape(-1, GROUP_SUBLANES, LANES), w1, w3, w2)
    return _combine(tabs, q, wts, y.reshape(-1, LANES), x2, final_g)


def kernel(x, mix_norm_g, w_in, conv_w, conv_b, conv_ln_g, conv_ln_b, conv_w_out, pool_w, pool_scale,
           sgu_ln_g, sgu_ln_b, sgu_w_s, sgu_b_s, sgu_w_out, w_out, ffn_norm_g, dense_w1, dense_w3,
           dense_w2, router_w, expert_w1, expert_w3, expert_w2, final_norm_g):
    b, s, d = x.shape

    def mixer(xx, l):
        return _mixer_layer(xx, mix_norm_g[l], w_in[l], conv_w[l], conv_b[l], conv_ln_g[l], conv_ln_b[l],
                            conv_w_out[l], pool_w[l], pool_scale[l], sgu_ln_g[l], sgu_ln_b[l],
                            sgu_w_s[l], sgu_b_s[l], sgu_w_out[l], w_out[l])

    x = mixer(x, 0)
    x2 = _dense_ffn(x.reshape(b * s, d), ffn_norm_g[0], dense_w1[0], dense_w3[0], dense_w2[0])
    x = mixer(x2.reshape(b, s, d), 1)
    out = _moe_and_final_norm(x.reshape(b * s, d), ffn_norm_g[1], router_w[0], expert_w1[0],
                              expert_w3[0], expert_w2[0], final_norm_g)
    return out.reshape(b, s, d)
```
